```python
import jax, jax.numpy as jnp
from jax import lax
import numpy as np

D_MODEL = 4096
BATCH = 4
SEQ = 4096
DEPTH = 2

CHUNK = 64
N_META = 16
Q_BLOCK = 128
RMS_EPS = 1e-6

LRU_WIDTH = D_MODEL // 2
LRU_BLOCKS = 16
LRU_BLOCK_DIM = LRU_WIDTH // LRU_BLOCKS
CONV_WIDTH = 4
LRU_C = 8.0
FOX_HEADS = 16
FOX_HEAD_DIM = (D_MODEL // 2) // FOX_HEADS
FOX_WIDTH = FOX_HEADS * FOX_HEAD_DIM
AB_IN = 2 * LRU_WIDTH + 3 * FOX_WIDTH + FOX_HEADS
AB_MIX = LRU_WIDTH + FOX_WIDTH

RET_HEADS = 16
RET_QK_DIM = D_MODEL // RET_HEADS
RET_V_DIM = 2 * D_MODEL // RET_HEADS
RET_QK_WIDTH = RET_HEADS * RET_QK_DIM
RET_V_WIDTH = RET_HEADS * RET_V_DIM
RET_IN = 2 * RET_QK_WIDTH + 2 * RET_V_WIDTH
ROPE_BASE = 10000.0

D_FF = -(-8 * D_MODEL // (3 * 256)) * 256

N_EVEN = (DEPTH + 1) // 2
N_ODD = DEPTH // 2

kernel_name = "hybrid_rglru_fox_retention_trunk"


def rms_norm(x, g):
    xf = x.astype(jnp.float32)
    y = xf * lax.rsqrt(jnp.mean(xf * xf, axis=-1, keepdims=True) + RMS_EPS)
    return (y * g.astype(jnp.float32)).astype(x.dtype)


def swiglu(x, w_gate, w_up, w_down):
    return (jax.nn.silu(x @ w_gate) * (x @ w_up)) @ w_down


def causal_depthwise_conv(x, w, b):
    L = x.shape[1]
    K = w.shape[0]
    xp = jnp.pad(x, ((0, 0), (K - 1, 0), (0, 0)))
    y = b
    for j in range(K):
        y = y + xp[:, j:j + L] * w[j]
    return y


def rg_lru(x, w_a, b_a, w_x, b_x, lam):
    Bsz, L, C = x.shape
    xb = x.reshape(Bsz, L, LRU_BLOCKS, LRU_BLOCK_DIM)
    r = jax.nn.sigmoid(jnp.einsum('blnc,ncd->blnd', xb, w_a).reshape(Bsz, L, C) + b_a)
    i = jax.nn.sigmoid(jnp.einsum('blnc,ncd->blnd', xb, w_x).reshape(Bsz, L, C) + b_x)
    log_a = -LRU_C * jax.nn.softplus(-lam.astype(jnp.float32)) * r.astype(jnp.float32)
    a = jnp.exp(log_a)
    gated_x = jnp.sqrt(-jnp.expm1(2.0 * log_a)) * (i * x).astype(jnp.float32)

    def combine(p, q):
        a1, b1 = p
        a2, b2 = q
        return a1 * a2, a2 * b1 + b2

    _, h = lax.associative_scan(combine, (a, gated_x), axis=1)
    return h.astype(x.dtype)


def forgetting_attention(q, k, v, f_logit):
    Bsz, L, H, Dh = q.shape
    log_f = jax.nn.log_sigmoid(f_logit.astype(jnp.float32))
    cum = jnp.cumsum(log_f, axis=1).transpose(0, 2, 1)
    n_blocks = -(-L // Q_BLOCK)
    pad = n_blocks * Q_BLOCK - L
    qp = jnp.pad(q, ((0, 0), (0, pad), (0, 0), (0, 0)))
    cq = jnp.pad(cum, ((0, 0), (0, 0), (0, pad)))
    scale = Dh ** -0.5
    key_pos = jnp.arange(L)

    def block(i):
        start = i * Q_BLOCK
        qb = lax.dynamic_slice_in_dim(qp, start, Q_BLOCK, axis=1)
        cb = lax.dynamic_slice_in_dim(cq, start, Q_BLOCK, axis=2)
        s = jnp.einsum('bqhd,bkhd->bhqk', qb, k).astype(jnp.float32) * scale
        s = s + cb[..., None] - cum[:, :, None, :]
        qpos = start + jnp.arange(Q_BLOCK)
        s = jnp.where(key_pos[None, :] <= qpos[:, None], s, -jnp.inf)
        p = jax.nn.softmax(s, axis=-1).astype(v.dtype)
        return jnp.einsum('bhqk,bkhd->bqhd', p, v)

    out = lax.map(block, jnp.arange(n_blocks))
    out = out.transpose(1, 0, 2, 3, 4).reshape(Bsz, n_blocks * Q_BLOCK, H, Dh)
    return out[:, :L]


def lru_fox_mixer(h, w_in, b_f, conv_w, conv_b, w_a, b_a, w_x, b_x, lam, q_norm, k_norm, w_out):
    Bsz, L, _ = h.shape
    z = h @ w_in
    cuts = [LRU_WIDTH, 2 * LRU_WIDTH, 2 * LRU_WIDTH + FOX_WIDTH,
            2 * LRU_WIDTH + 2 * FOX_WIDTH, 2 * LRU_WIDTH + 3 * FOX_WIDTH]
    x_lru, gate, q, k, v, f = jnp.split(z, cuts, axis=-1)
    x_lru = causal_depthwise_conv(x_lru, conv_w, conv_b)
    y_lru = rg_lru(x_lru, w_a, b_a, w_x, b_x, lam) * jax.nn.gelu(gate)
    q = rms_norm(q.reshape(Bsz, L, FOX_HEADS, FOX_HEAD_DIM), q_norm)
    k = rms_norm(k.reshape(Bsz, L, FOX_HEADS, FOX_HEAD_DIM), k_norm)
    v = v.reshape(Bsz, L, FOX_HEADS, FOX_HEAD_DIM)
    y_fox = forgetting_attention(q, k, v, f + b_f).reshape(Bsz, L, FOX_WIDTH)
    return jnp.concatenate([y_lru, y_fox.astype(y_lru.dtype)], axis=-1) @ w_out


def rotary(x, pos):
    half = x.shape[-1] // 2
    inv = ROPE_BASE ** (-jnp.arange(half, dtype=jnp.float32) / half)
    ang = pos[:, None].astype(jnp.float32) * inv[None, :]
    cos = jnp.cos(ang)[None, :, None, :]
    sin = jnp.sin(ang)[None, :, None, :]
    xf = x.astype(jnp.float32)
    x1, x2 = xf[..., :half], xf[..., half:]
    return jnp.concatenate([x1 * cos - x2 * sin, x1 * sin + x2 * cos], axis=-1).astype(x.dtype)


def retention_mixer(h, w_in, ret_norm, w_out):
    Bsz, L, _ = h.shape
    z = h @ w_in
    q, k, v, g = jnp.split(z, [RET_QK_WIDTH, 2 * RET_QK_WIDTH, 2 * RET_QK_WIDTH + RET_V_WIDTH], axis=-1)
    pos = jnp.arange(L)
    q = rotary(q.reshape(Bsz, L, RET_HEADS, RET_QK_DIM), pos)
    k = rotary(k.reshape(Bsz, L, RET_HEADS, RET_QK_DIM), pos) * (RET_QK_DIM ** -0.5)
    v = v.reshape(Bsz, L, RET_HEADS, RET_V_DIM)
    lead = (-N_META) % CHUNK
    padw = ((0, 0), (lead, 0), (0, 0), (0, 0))
    Lp = L + lead
    n_chunks = Lp // CHUNK

    def to_chunks(t):
        return jnp.pad(t, padw).reshape(Bsz, n_chunks, CHUNK, RET_HEADS, -1).transpose(1, 0, 3, 2, 4)

    qc, kc, vc = to_chunks(q), to_chunks(k), to_chunks(v)
    log_gamma = jnp.log(1.0 - 2.0 ** (-5.0 - jnp.arange(RET_HEADS, dtype=jnp.float32)))
    idx = jnp.arange(CHUNK, dtype=jnp.float32)
    intra_decay = jnp.exp(log_gamma[:, None, None] * jnp.abs(idx[:, None] - idx[None, :]))
    q_decay = jnp.exp(log_gamma[:, None] * (idx + 1.0))[..., None]
    k_decay = jnp.exp(log_gamma[:, None] * (CHUNK - 1.0 - idx))[..., None]
    chunk_decay = jnp.exp(log_gamma * CHUNK)[:, None, None]

    def step(S, inp):
        qb, kb, vb = inp
        s = jnp.einsum('bhcd,bhmd->bhcm', qb, kb) * intra_decay
        o = jnp.einsum('bhcm,bhme->bhce', s, vb) + jnp.einsum('bhcd,bhde->bhce', qb * q_decay, S)
        S = S * chunk_decay + jnp.einsum('bhmd,bhme->bhde', kb * k_decay, vb)
        return S, o

    S0 = jnp.zeros((Bsz, RET_HEADS, RET_QK_DIM, RET_V_DIM), jnp.float32)
    _, o = lax.scan(step, S0, (qc, kc, vc))
    o = o.transpose(1, 0, 3, 2, 4).reshape(Bsz, Lp, RET_HEADS, RET_V_DIM)[:, lead:]
    o = rms_norm(o, ret_norm.reshape(RET_HEADS, RET_V_DIM)).reshape(Bsz, L, RET_V_WIDTH).astype(h.dtype)
    return (jax.nn.silu(g) * o) @ w_out


def setup_inputs(seed: int = 0) -> dict:
    key = jax.random.key(seed)
    ks = jax.random.split(key, 24)
    f32 = jnp.float32

    def nrm(k, shape, fan_in):
        return jax.random.normal(k, shape, f32) * (fan_in ** -0.5)

    def gain(k, shape):
        return 1.0 + 0.01 * jax.random.normal(k, shape, f32)

    def small(k, shape):
        return 0.01 * jax.random.normal(k, shape, f32)

    a0 = jax.random.uniform(ks[11], (N_EVEN, LRU_WIDTH), f32, 0.9, 0.999)
    p = a0 ** (1.0 / LRU_C)
    lam = jnp.log(p) - jnp.log1p(-p)
    return {
        "x": jax.random.normal(ks[0], (BATCH, SEQ, D_MODEL), f32),
        "meta_tokens": jax.random.normal(ks[1], (N_META, D_MODEL), f32),
        "ab_norm": gain(ks[2], (N_EVEN, D_MODEL)),
        "ab_w_in": nrm(ks[3], (N_EVEN, D_MODEL, AB_IN), D_MODEL),
        "ab_b_f": 2.0 + 0.1 * jax.random.normal(ks[4], (N_EVEN, FOX_HEADS), f32),
        "ab_conv_w": nrm(ks[5], (N_EVEN, CONV_WIDTH, LRU_WIDTH), CONV_WIDTH),
        "ab_conv_b": small(ks[6], (N_EVEN, LRU_WIDTH)),
        "ab_w_a": nrm(ks[7], (N_EVEN, LRU_BLOCKS, LRU_BLOCK_DIM, LRU_BLOCK_DIM), LRU_BLOCK_DIM),
        "ab_b_a": small(ks[8], (N_EVEN, LRU_WIDTH)),
        "ab_w_x": nrm(ks[9], (N_EVEN, LRU_BLOCKS, LRU_BLOCK_DIM, LRU_BLOCK_DIM), LRU_BLOCK_DIM),
        "ab_b_x": small(ks[10], (N_EVEN, LRU_WIDTH)),
        "ab_lambda": lam,
        "ab_q_norm": gain(ks[12], (N_EVEN, FOX_HEAD_DIM)),
        "ab_k_norm": gain(ks[13], (N_EVEN, FOX_HEAD_DIM)),
        "ab_w_out": nrm(ks[14], (N_EVEN, AB_MIX, D_MODEL), AB_MIX),
        "c_norm": gain(ks[15], (N_ODD, D_MODEL)),
        "c_w_in": nrm(ks[16], (N_ODD, D_MODEL, RET_IN), D_MODEL),
        "c_ret_norm": gain(ks[17], (N_ODD, RET_V_WIDTH)),
        "c_w_out": nrm(ks[18], (N_ODD, RET_V_WIDTH, D_MODEL), RET_V_WIDTH),
        "ffn_norm": gain(ks[19], (DEPTH, D_MODEL)),
        "ffn_w_gate": nrm(ks[20], (DEPTH, D_MODEL, D_FF), D_MODEL),
        "ffn_w_up": nrm(ks[21], (DEPTH, D_MODEL, D_FF), D_MODEL),
        "ffn_w_down": nrm(ks[22], (DEPTH, D_FF, D_MODEL), D_FF),
    }


def reference(x, meta_tokens, ab_norm, ab_w_in, ab_b_f, ab_conv_w, ab_conv_b, ab_w_a, ab_b_a,
              ab_w_x, ab_b_x, ab_lambda, ab_q_norm, ab_k_norm, ab_w_out, c_norm, c_w_in,
              c_ret_norm, c_w_out, ffn_norm, ffn_w_gate, ffn_w_up, ffn_w_down):
    Bsz = x.shape[0]
    meta = jnp.broadcast_to(meta_tokens[None].astype(x.dtype), (Bsz, N_META, D_MODEL))
    h = jnp.concatenate([meta, x], axis=1)
    for layer in range(DEPTH):
        j = layer // 2
        if layer % 2 == 0:
            h = h + lru_fox_mixer(rms_norm(h, ab_norm[j]), ab_w_in[j], ab_b_f[j], ab_conv_w[j],
                                  ab_conv_b[j], ab_w_a[j], ab_b_a[j], ab_w_x[j], ab_b_x[j],
                                  ab_lambda[j], ab_q_norm[j], ab_k_norm[j], ab_w_out[j])
        else:
            h = h + retention_mixer(rms_norm(h, c_norm[j]), c_w_in[j], c_ret_norm[j], c_w_out[j])
        h = h + swiglu(rms_norm(h, ffn_norm[layer]), ffn_w_gate[layer], ffn_w_up[layer], ffn_w_down[layer])
    return h[:, N_META:]
```

```python
import functools

import jax
import jax.numpy as jnp
from jax import lax
from jax.experimental import pallas as pl
from jax.experimental.pallas import tpu as pltpu

F32 = jnp.float32
BF16 = jnp.bfloat16

LANE = 128
SUBLANE = 8
VMEM_LIMIT_BYTES = 56 * 1024 * 1024

RMS_EPS = 1e-6
LRU_C = 8.0
RET_CHUNK = 64
RET_HEADS = 16
ROPE_BASE = 10000.0
MASK_VALUE = -1e30


def _divisor(n, candidates):
    for c in candidates:
        if n % c == 0:
            return c
    raise ValueError(f"no tile in {candidates} divides {n}")


def _params(*semantics):
    return pltpu.CompilerParams(dimension_semantics=semantics,
                                vmem_limit_bytes=VMEM_LIMIT_BYTES)


def _rmsnorm_kernel(x_ref, g_ref, o_ref):
    x = x_ref[...]
    ms = jnp.mean(x * x, axis=-1, keepdims=True)
    o_ref[...] = (x * lax.rsqrt(ms + RMS_EPS) * g_ref[...]).astype(o_ref.dtype)


def rmsnorm(x, gain):
    m, d = x.shape
    tr = _divisor(m, (256, 128, 64, 8))
    return pl.pallas_call(
        _rmsnorm_kernel,
        out_shape=jax.ShapeDtypeStruct((m, d), BF16),
        grid=(m // tr,),
        in_specs=[pl.BlockSpec((tr, d), lambda i: (i, 0)),
                  pl.BlockSpec((1, d), lambda i: (0, 0))],
        out_specs=pl.BlockSpec((tr, d), lambda i: (i, 0)),
        compiler_params=_params("parallel"),
        name="rmsnorm",
    )(x, gain.reshape(1, d).astype(F32))


def _matmul_tiles(m, k, n):
    if k <= 4096:
        tm = _divisor(m, (1536, 768, 512, 256, 128, 8))
        tn = _divisor(n, (512, 256, 128))
    elif k <= 8192:
        tm = _divisor(m, (768, 512, 256, 128, 8))
        tn = _divisor(n, (512, 256, 128))
    else:
        tm = _divisor(m, (768, 512, 256, 128, 8))
        tn = _divisor(n, (256, 128))
    return tm, tn


def _mm_kernel(x_ref, w_ref, o_ref):
    o_ref[...] = jnp.dot(x_ref[...], w_ref[...],
                         preferred_element_type=F32).astype(o_ref.dtype)


def matmul(x, w, name):
    m, k = x.shape
    n = w.shape[1]
    tm, tn = _matmul_tiles(m, k, n)
    return pl.pallas_call(
        _mm_kernel,
        out_shape=jax.ShapeDtypeStruct((m, n), BF16),
        grid=(m // tm, n // tn),
        in_specs=[pl.BlockSpec((tm, k), lambda i, j: (i, 0)),
                  pl.BlockSpec((k, tn), lambda i, j: (0, j))],
        out_specs=pl.BlockSpec((tm, tn), lambda i, j: (i, j)),
        compiler_params=_params("parallel", "arbitrary"),
        name=name,
    )(x, w)


def _mm_residual_kernel(x_ref, w_ref, r_ref, o_ref):
    o_ref[...] = r_ref[...] + jnp.dot(x_ref[...], w_ref[...], preferred_element_type=F32)


def matmul_residual(x, w, res, name):
    m, k = x.shape
    n = w.shape[1]
    tm, tn = _matmul_tiles(m, k, n)
    return pl.pallas_call(
        _mm_residual_kernel,
        out_shape=jax.ShapeDtypeStruct((m, n), F32),
        grid=(m // tm, n // tn),
        in_specs=[pl.BlockSpec((tm, k), lambda i, j: (i, 0)),
                  pl.BlockSpec((k, tn), lambda i, j: (0, j)),
                  pl.BlockSpec((tm, tn), lambda i, j: (i, j))],
        out_specs=pl.BlockSpec((tm, tn), lambda i, j: (i, j)),
        compiler_params=_params("parallel", "arbitrary"),
        name=name,
    )(x, w, res)


def _mm2_residual_kernel(xa_ref, xb_ref, wa_ref, wb_ref, r_ref, o_ref):
    acc = jnp.dot(xa_ref[...], wa_ref[...], preferred_element_type=F32)
    acc = acc + jnp.dot(xb_ref[...], wb_ref[...], preferred_element_type=F32)
    o_ref[...] = r_ref[...] + acc


def matmul2_residual(xa, xb, w, res, name):
    m, ka = xa.shape
    kb = xb.shape[1]
    assert ka == kb and w.shape[0] == ka + kb
    n = w.shape[1]
    tm, tn = _matmul_tiles(m, ka + kb, n)
    return pl.pallas_call(
        _mm2_residual_kernel,
        out_shape=jax.ShapeDtypeStruct((m, n), F32),
        grid=(m // tm, n // tn),
        in_specs=[pl.BlockSpec((tm, ka), lambda i, j: (i, 0)),
                  pl.BlockSpec((tm, kb), lambda i, j: (i, 0)),
                  pl.BlockSpec((ka, tn), lambda i, j: (0, j)),
                  pl.BlockSpec((kb, tn), lambda i, j: (1, j)),
                  pl.BlockSpec((tm, tn), lambda i, j: (i, j))],
        out_specs=pl.BlockSpec((tm, tn), lambda i, j: (i, j)),
        compiler_params=_params("parallel", "arbitrary"),
        name=name,
    )(xa, xb, w, w, res)


def _swiglu_up_kernel(x_ref, wg_ref, wu_ref, o_ref):
    x = x_ref[...]
    g = jnp.dot(x, wg_ref[...], preferred_element_type=F32)
    u = jnp.dot(x, wu_ref[...], preferred_element_type=F32)
    o_ref[...] = (g * jax.nn.sigmoid(g) * u).astype(o_ref.dtype)


def swiglu_up(x, w_gate, w_up):
    m, k = x.shape
    n = w_gate.shape[1]
    tm = _divisor(m, (1536, 768, 512, 256, 128, 8))
    tn = _divisor(n, (256, 128))
    return pl.pallas_call(
        _swiglu_up_kernel,
        out_shape=jax.ShapeDtypeStruct((m, n), BF16),
        grid=(m // tm, n // tn),
        in_specs=[pl.BlockSpec((tm, k), lambda i, j: (i, 0)),
                  pl.BlockSpec((k, tn), lambda i, j: (0, j)),
                  pl.BlockSpec((k, tn), lambda i, j: (0, j))],
        out_specs=pl.BlockSpec((tm, tn), lambda i, j: (i, j)),
        compiler_params=_params("parallel", "arbitrary"),
        name="swiglu_up",
    )(x, w_gate, w_up)


def _headnorm_mm_kernel(x_ref, w_ref, g_ref, o_ref, *, norm_lo, norm_hi, head_dim):
    j = pl.program_id(1)
    acc = jnp.dot(x_ref[...], w_ref[...], preferred_element_type=F32)
    is_norm = jnp.logical_and(j >= norm_lo, j < norm_hi)

    @pl.when(is_norm)
    def _():
        for h in range(acc.shape[1] // head_dim):
            cols = slice(h * head_dim, (h + 1) * head_dim)
            y = acc[:, cols]
            ms = jnp.mean(y * y, axis=-1, keepdims=True)
            o_ref[:, cols] = (y * lax.rsqrt(ms + RMS_EPS) * g_ref[:, cols]).astype(o_ref.dtype)

    @pl.when(jnp.logical_not(is_norm))
    def _():
        o_ref[...] = acc.astype(o_ref.dtype)


def headnorm_matmul(x, w, gains, norm_cols, head_dim, name):
    m, k = x.shape
    n = w.shape[1]
    tm, tn = _matmul_tiles(m, k, n)
    assert tn % head_dim == 0 and norm_cols[0] % tn == 0 and norm_cols[1] % tn == 0
    kern = functools.partial(_headnorm_mm_kernel, norm_lo=norm_cols[0] // tn,
                             norm_hi=norm_cols[1] // tn, head_dim=head_dim)
    return pl.pallas_call(
        kern,
        out_shape=jax.ShapeDtypeStruct((m, n), BF16),
        grid=(m // tm, n // tn),
        in_specs=[pl.BlockSpec((tm, k), lambda i, j: (i, 0)),
                  pl.BlockSpec((k, tn), lambda i, j: (0, j)),
                  pl.BlockSpec((1, tn), lambda i, j: (0, j))],
        out_specs=pl.BlockSpec((tm, tn), lambda i, j: (i, j)),
        compiler_params=_params("parallel", "arbitrary"),
        name=name,
    )(x, w, gains)


def _forget_logit_kernel(x_ref, w_ref, b_ref, o_ref):
    f = jnp.dot(x_ref[...], w_ref[...], preferred_element_type=F32) + b_ref[...]
    o_ref[...] = jnp.minimum(f, 0.0) - jnp.log1p(jnp.exp(-jnp.abs(f)))


def forget_log_gates(x, w_f, b_f):
    m, k = x.shape
    n = w_f.shape[1]
    tm = _divisor(m, (1536, 768, 512, 256, 128, 8))
    return pl.pallas_call(
        _forget_logit_kernel,
        out_shape=jax.ShapeDtypeStruct((m, n), F32),
        grid=(m // tm,),
        in_specs=[pl.BlockSpec((tm, k), lambda i: (i, 0)),
                  pl.BlockSpec((k, n), lambda i: (0, 0)),
                  pl.BlockSpec((1, n), lambda i: (0, 0))],
        out_specs=pl.BlockSpec((tm, n), lambda i: (i, 0)),
        compiler_params=_params("parallel"),
        name="fox_forget_gates",
    )(x, w_f, b_f)


def _prefix_sum_kernel(x_ref, o_ref, *, lead):
    x = x_ref[0]
    rows = x.shape[0]
    row = lax.broadcasted_iota(jnp.int32, x.shape, 0)
    x = jnp.where(row >= lead, x, 0.0)
    shift = 1
    while shift < rows:
        x = x + jnp.where(row >= shift, pltpu.roll(x, shift, 0), 0.0)
        shift *= 2
    o_ref[0] = x


def prefix_sum_time(x, lead):
    b, rows, lanes = x.shape
    return pl.pallas_call(
        functools.partial(_prefix_sum_kernel, lead=lead),
        out_shape=jax.ShapeDtypeStruct(x.shape, F32),
        grid=(b,),
        in_specs=[pl.BlockSpec((1, rows, lanes), lambda i: (i, 0, 0))],
        out_specs=pl.BlockSpec((1, rows, lanes), lambda i: (i, 0, 0)),
        compiler_params=_params("parallel"),
        name="fox_prefix_sum",
    )(x)


def _fox_kernel(q_ref, k_ref, v_ref, cum_ref, ckh_ref, ckm_ref, o_ref,
                m_ref, l_ref, acc_ref, *, lead, head_rows, tq):
    h = pl.program_id(1)
    rows = q_ref.shape[1]
    n_main = (rows - head_rows) // tq
    lane = lax.broadcasted_iota(jnp.int32, (1, cum_ref.shape[2]), 1)

    def query_bias(r0, nrows):
        c = cum_ref[0, pl.ds(r0, nrows), :]
        return jnp.sum(jnp.where(lane == h, c, 0.0), axis=-1, keepdims=True)

    def attend(q, cq, k, v, ck, mask):
        s = lax.dot_general(q, k, (((1,), (1,)), ((), ())), preferred_element_type=F32)
        s = s + cq - ck
        if mask is not None:
            s = jnp.where(mask, s, MASK_VALUE)
        m_prev = m_ref[pl.ds(0, q.shape[0]), :]
        m_new = jnp.maximum(m_prev, jnp.max(s, axis=-1, keepdims=True))
        alpha = jnp.exp(m_prev - m_new)
        p = jnp.exp(s - m_new)
        l_ref[pl.ds(0, q.shape[0]), :] = (alpha * l_ref[pl.ds(0, q.shape[0]), :]
                                          + jnp.sum(p, axis=-1, keepdims=True))
        acc_ref[pl.ds(0, q.shape[0]), :] = (
            alpha * acc_ref[pl.ds(0, q.shape[0]), :]
            + jnp.dot(p.astype(v.dtype), v, preferred_element_type=F32))
        m_ref[pl.ds(0, q.shape[0]), :] = m_new

    def reset(nrows):
        m_ref[pl.ds(0, nrows), :] = jnp.full((nrows, 1), MASK_VALUE, F32)
        l_ref[pl.ds(0, nrows), :] = jnp.zeros((nrows, 1), F32)
        acc_ref[pl.ds(0, nrows), :] = jnp.zeros((nrows, acc_ref.shape[1]), F32)

    def finish(r0, nrows):
        o_ref[0, pl.ds(r0, nrows), :] = (acc_ref[pl.ds(0, nrows), :]
                                         / l_ref[pl.ds(0, nrows), :]).astype(o_ref.dtype)

    k_head = k_ref[0, pl.ds(0, head_rows), :]
    v_head = v_ref[0, pl.ds(0, head_rows), :]
    ck_head = ckh_ref[0, 0]
    key_h = lax.broadcasted_iota(jnp.int32, (1, head_rows), 1)
    qrow_h = lax.broadcasted_iota(jnp.int32, (head_rows, 1), 0)

    reset(head_rows)
    attend(q_ref[0, pl.ds(0, head_rows), :], query_bias(0, head_rows), k_head, v_head, ck_head,
           jnp.logical_and(key_h >= lead, key_h <= qrow_h))
    finish(0, head_rows)

    diag = (lax.broadcasted_iota(jnp.int32, (tq, tq), 1)
            <= lax.broadcasted_iota(jnp.int32, (tq, tq), 0))

    def q_tile(i, carry):
        r0 = pl.multiple_of(head_rows + i * tq, LANE)
        q = q_ref[0, pl.ds(r0, tq), :]
        cq = query_bias(r0, tq)
        reset(tq)
        attend(q, cq, k_head, v_head, ck_head, key_h >= lead)

        def kv_tile(j, c):
            k0 = pl.multiple_of(head_rows + j * tq, LANE)
            attend(q, cq, k_ref[0, pl.ds(k0, tq), :], v_ref[0, pl.ds(k0, tq), :],
                   ckm_ref[0, 0, j], None)
            return c

        lax.fori_loop(0, i, kv_tile, 0)
        attend(q, cq, k_ref[0, pl.ds(r0, tq), :], v_ref[0, pl.ds(r0, tq), :],
               ckm_ref[0, 0, i], diag)
        finish(r0, tq)
        return carry

    lax.fori_loop(0, n_main, q_tile, 0)


def fox_attention(z, cum, cum_t, *, batch, rows, lead, head_rows, heads, head_dim,
                  q_col, k_col, v_col):
    z3 = z.reshape(batch, rows, z.shape[1])
    tq = _divisor(rows - head_rows, (512, 256, 128))
    n_main = (rows - head_rows) // tq
    ck_head = cum_t[:, :, :head_rows].reshape(batch, heads, 1, head_rows)
    ck_main = cum_t[:, :, head_rows:].reshape(batch, heads, n_main, 1, tq)
    qb, kb, vb = q_col // head_dim, k_col // head_dim, v_col // head_dim
    kern = functools.partial(_fox_kernel, lead=lead, head_rows=head_rows, tq=tq)
    tmax = max(tq, head_rows)
    return pl.pallas_call(
        kern,
        out_shape=jax.ShapeDtypeStruct((batch, rows, heads * head_dim), BF16),
        grid=(batch, heads),
        in_specs=[pl.BlockSpec((1, rows, head_dim), lambda b, h: (b, 0, qb + h)),
                  pl.BlockSpec((1, rows, head_dim), lambda b, h: (b, 0, kb + h)),
                  pl.BlockSpec((1, rows, head_dim), lambda b, h: (b, 0, vb + h)),
                  pl.BlockSpec((1, rows, cum.shape[2]), lambda b, h: (b, 0, 0)),
                  pl.BlockSpec((1, 1, 1, head_rows), lambda b, h: (b, h, 0, 0)),
                  pl.BlockSpec((1, 1, n_main, 1, tq), lambda b, h: (b, h, 0, 0, 0))],
        out_specs=pl.BlockSpec((1, rows, head_dim), lambda b, h: (b, 0, h)),
        scratch_shapes=[pltpu.VMEM((tmax, 1), F32), pltpu.VMEM((tmax, 1), F32),
                        pltpu.VMEM((tmax, head_dim), F32)],
        compiler_params=_params("parallel", "arbitrary"),
        name="fox_attention",
    )(z3, z3, z3, cum, ck_head, ck_main)


def _lru_kernel(x_ref, gate_ref, cw_ref, cb_ref, wa_ref, ba_ref, wx_ref, bx_ref, lam_ref,
                o_ref, ext_ref, a_ref, b_ref, carry_ref, *, lead, block_dim):
    t = pl.program_id(2)
    tl, tc = x_ref.shape[1], x_ref.shape[2]
    taps = cw_ref.shape[0]

    @pl.when(t == 0)
    def _():
        ext_ref[pl.ds(0, SUBLANE), :] = jnp.zeros((SUBLANE, tc), F32)
        carry_ref[...] = jnp.zeros_like(carry_ref)

    row = t * tl + lax.broadcasted_iota(jnp.int32, (tl, 1), 0)
    valid = row >= lead
    x = jnp.where(valid, x_ref[0].astype(F32), 0.0)
    ext_ref[pl.ds(SUBLANE, tl), :] = x

    xc = cb_ref[...] + ext_ref[pl.ds(SUBLANE - (taps - 1), tl), :] * cw_ref[pl.ds(0, 1), :]
    for j in range(1, taps):
        xc = xc + ext_ref[pl.ds(SUBLANE - (taps - 1) + j, tl), :] * cw_ref[pl.ds(j, 1), :]
    ext_ref[pl.ds(0, SUBLANE), :] = ext_ref[pl.ds(tl, SUBLANE), :]

    xcb = xc.astype(BF16)
    lam = lam_ref[...]
    softplus_neg_lam = jnp.maximum(-lam, 0.0) + jnp.log1p(jnp.exp(-jnp.abs(lam)))
    rate = -LRU_C * softplus_neg_lam
    for g in range(tc // block_dim):
        cols = slice(g * block_dim, (g + 1) * block_dim)
        xg = xcb[:, cols]
        r = jax.nn.sigmoid(jnp.dot(xg, wa_ref[g], preferred_element_type=F32) + ba_ref[:, cols])
        i = jax.nn.sigmoid(jnp.dot(xg, wx_ref[g], preferred_element_type=F32) + bx_ref[:, cols])
        log_a = rate[:, cols] * r
        a = jnp.exp(log_a)
        gated = jnp.sqrt(1.0 - jnp.exp(2.0 * log_a)) * (i * xc[:, cols])
        a_ref[:, cols] = a
        b_ref[:, cols] = jnp.where(valid, gated, 0.0)

    a = a_ref[...]
    b = b_ref[...]
    sub = lax.broadcasted_iota(jnp.int32, (tl, 1), 0) % SUBLANE
    shift = 1
    while shift < SUBLANE:
        keep = sub >= shift
        a_prev = jnp.where(keep, pltpu.roll(a, shift, 0), 1.0)
        b_prev = jnp.where(keep, pltpu.roll(b, shift, 0), 0.0)
        b = a * b_prev + b
        a = a * a_prev
        shift *= 2
    a_ref[...] = a
    b_ref[...] = b

    def group(gidx, h_prev):
        r0 = pl.multiple_of(gidx * SUBLANE, SUBLANE)
        hg = a_ref[pl.ds(r0, SUBLANE), :] * h_prev + b_ref[pl.ds(r0, SUBLANE), :]
        b_ref[pl.ds(r0, SUBLANE), :] = hg
        return hg[SUBLANE - 1:SUBLANE, :]

    carry_ref[...] = lax.fori_loop(0, tl // SUBLANE, group, carry_ref[...], unroll=4)

    gate = gate_ref[0].astype(F32)
    o_ref[0] = (b_ref[...] * jax.nn.gelu(gate)).astype(o_ref.dtype)


def lru_branch(z, conv_w, conv_b, w_a, b_a, w_x, b_x, lam, *, batch, rows, lead, width):
    z3 = z.reshape(batch, rows, z.shape[1])
    nblk, bd, _ = w_a.shape
    tc = _divisor(width, (512, 256, 128))
    tl = _divisor(rows, (384, 256, 128, 64, 8))
    nct = width // tc
    row2 = lambda v: v.reshape(1, width).astype(F32)
    kern = functools.partial(_lru_kernel, lead=lead, block_dim=bd)
    vec = pl.BlockSpec((1, tc), lambda b, c, t: (0, c))
    return pl.pallas_call(
        kern,
        out_shape=jax.ShapeDtypeStruct((batch, rows, width), BF16),
        grid=(batch, nct, rows // tl),
        in_specs=[pl.BlockSpec((1, tl, tc), lambda b, c, t: (b, t, c)),
                  pl.BlockSpec((1, tl, tc), lambda b, c, t: (b, t, nct + c)),
                  pl.BlockSpec((conv_w.shape[0], tc), lambda b, c, t: (0, c)),
                  vec,
                  pl.BlockSpec((tc // bd, bd, bd), lambda b, c, t: (c, 0, 0)),
                  vec,
                  pl.BlockSpec((tc // bd, bd, bd), lambda b, c, t: (c, 0, 0)),
                  vec, vec],
        out_specs=pl.BlockSpec((1, tl, tc), lambda b, c, t: (b, t, c)),
        scratch_shapes=[pltpu.VMEM((tl + SUBLANE, tc), F32), pltpu.VMEM((tl, tc), F32),
                        pltpu.VMEM((tl, tc), F32), pltpu.VMEM((1, tc), F32)],
        compiler_params=_params("parallel", "parallel", "arbitrary"),
        name="rg_lru",
    )(z3, z3, conv_w.astype(F32), row2(conv_b), w_a.astype(BF16), row2(b_a),
      w_x.astype(BF16), row2(b_x), row2(lam))


def _retention_kernel(lg_ref, q_ref, k_ref, v_ref, g_ref, cos_ref, sin_ref, gain_ref, o_ref,
                      s_ref, d_ref, *, lead, tb, chunk):
    h = pl.program_id(1)
    rows = q_ref.shape[1]
    dk = q_ref.shape[2]
    half = dk // 2
    lg = lg_ref[h]

    ti = lax.broadcasted_iota(jnp.int32, (tb, tb), 0)
    si = lax.broadcasted_iota(jnp.int32, (tb, tb), 1)
    dist = jnp.abs(ti - si).astype(F32)
    d_ref[...] = jnp.where(si // chunk <= ti // chunk, jnp.exp(lg * dist), 0.0)
    idx = lax.broadcasted_iota(jnp.int32, (tb, 1), 0).astype(F32)
    q_decay = jnp.exp(lg * (idx + 1.0))
    k_decay = jnp.exp(lg * (tb - 1.0 - idx))
    block_decay = jnp.exp(lg * jnp.full((1, 1), float(tb), F32))
    s_ref[...] = jnp.zeros_like(s_ref)

    def rotate(x, c, s):
        x1, x2 = x[:, :half], x[:, half:]
        return jnp.concatenate([x1 * c - x2 * s, x1 * s + x2 * c], axis=1)

    def block(n, carry):
        r0 = pl.multiple_of(n * tb, LANE)
        valid = (r0 + lax.broadcasted_iota(jnp.int32, (tb, 1), 0)) >= lead
        c = cos_ref[pl.ds(r0, tb), :]
        s = sin_ref[pl.ds(r0, tb), :]
        q = rotate(q_ref[0, pl.ds(r0, tb), :].astype(F32), c, s)
        k = rotate(k_ref[0, pl.ds(r0, tb), :].astype(F32), c, s) * (dk ** -0.5)
        k = jnp.where(valid, k, 0.0)
        v = v_ref[0, pl.ds(r0, tb), :]
        v = jnp.where(valid, v, jnp.zeros_like(v))

        w = lax.dot_general(q.astype(BF16), k.astype(BF16), (((1,), (1,)), ((), ())),
                            preferred_element_type=F32) * d_ref[...]
        state = s_ref[...]
        o = jnp.dot(w.astype(BF16), v, preferred_element_type=F32)
        o = o + jnp.dot((q * q_decay).astype(BF16), state.astype(BF16),
                        preferred_element_type=F32)
        s_ref[...] = state * block_decay + lax.dot_general(
            (k * k_decay).astype(BF16), v, (((0,), (0,)), ((), ())),
            preferred_element_type=F32)

        ms = jnp.mean(o * o, axis=-1, keepdims=True)
        on = o * lax.rsqrt(ms + RMS_EPS) * gain_ref[0]
        g = g_ref[0, pl.ds(r0, tb), :].astype(F32)
        o_ref[0, pl.ds(r0, tb), :] = (g * jax.nn.sigmoid(g) * on).astype(o_ref.dtype)
        return carry

    lax.fori_loop(0, rows // tb, block, 0)


def retention(z, cos, sin, ret_norm, *, batch, rows, lead, heads):
    cols = z.shape[1]
    z3 = z.reshape(batch, rows, cols)
    vw = ret_norm.shape[0]
    qk = (cols - 2 * vw) // 2
    dk, dv = qk // heads, vw // heads
    tb = _divisor(rows, (384, 128, 64))
    assert tb % RET_CHUNK == 0
    log_gamma = jnp.log(1.0 - 2.0 ** (-5.0 - jnp.arange(heads, dtype=F32)))
    kern = functools.partial(_retention_kernel, lead=lead, tb=tb, chunk=RET_CHUNK)
    kb, vb, gb = qk // dk, 2 * qk // dv, (2 * qk + vw) // dv
    return pl.pallas_call(
        kern,
        out_shape=jax.ShapeDtypeStruct((batch, rows, vw), BF16),
        grid=(batch, heads),
        in_specs=[pl.BlockSpec(memory_space=pltpu.SMEM),
                  pl.BlockSpec((1, rows, dk), lambda b, h: (b, 0, h)),
                  pl.BlockSpec((1, rows, dk), lambda b, h: (b, 0, kb + h)),
                  pl.BlockSpec((1, rows, dv), lambda b, h: (b, 0, vb + h)),
                  pl.BlockSpec((1, rows, dv), lambda b, h: (b, 0, gb + h)),
                  pl.BlockSpec((rows, dk // 2), lambda b, h: (0, 0)),
                  pl.BlockSpec((rows, dk // 2), lambda b, h: (0, 0)),
                  pl.BlockSpec((1, 1, dv), lambda b, h: (h, 0, 0))],
        out_specs=pl.BlockSpec((1, rows, dv), lambda b, h: (b, 0, h)),
        scratch_shapes=[pltpu.VMEM((dk, dv), F32), pltpu.VMEM((tb, tb), F32)],
        compiler_params=_params("parallel", "arbitrary"),
        name="retention",
    )(log_gamma, z3, z3, z3, z3, cos, sin, ret_norm.reshape(heads, 1, dv).astype(F32))


def _ffn(h, norm, w_gate, w_up, w_down):
    xn = rmsnorm(h, norm)
    a = swiglu_up(xn, w_gate.astype(BF16), w_up.astype(BF16))
    return matmul_residual(a, w_down.astype(BF16), h, "ffn_down")


def _lru_fox_layer(h, geom, norm, w_in, b_f, conv_w, conv_b, w_a, b_a, w_x, b_x, lam,
                   q_norm, k_norm, w_out):
    batch, rows, lead, head_rows = geom
    heads = b_f.shape[0]
    head_dim = q_norm.shape[0]
    fox_w = heads * head_dim
    lru_w = lam.shape[0]
    q_col, k_col, v_col, f_col = 2 * lru_w, 2 * lru_w + fox_w, 2 * lru_w + 2 * fox_w, 2 * lru_w + 3 * fox_w

    xn = rmsnorm(h, norm)
    gains = jnp.ones((1, f_col), F32)
    gains = gains.at[0, q_col:k_col].set(jnp.tile(q_norm.astype(F32) * head_dim ** -0.5, heads))
    gains = gains.at[0, k_col:v_col].set(jnp.tile(k_norm.astype(F32), heads))
    z = headnorm_matmul(xn, w_in[:, :f_col].astype(BF16), gains, (q_col, v_col), head_dim,
                        "lru_fox_in_proj")

    w_f = jnp.zeros((w_in.shape[0], LANE), BF16).at[:, :heads].set(w_in[:, f_col:].astype(BF16))
    bias_f = jnp.zeros((1, LANE), F32).at[0, :heads].set(b_f.astype(F32))
    log_f = forget_log_gates(xn, w_f, bias_f).reshape(batch, rows, LANE)
    cum = prefix_sum_time(log_f, lead)
    cum_t = jnp.transpose(cum[:, :, :heads], (0, 2, 1))

    y_lru = lru_branch(z, conv_w, conv_b, w_a, b_a, w_x, b_x, lam,
                       batch=batch, rows=rows, lead=lead, width=lru_w)
    y_fox = fox_attention(z, cum, cum_t, batch=batch, rows=rows, lead=lead, head_rows=head_rows,
                          heads=heads, head_dim=head_dim, q_col=q_col, k_col=k_col, v_col=v_col)
    m = batch * rows
    return matmul2_residual(y_lru.reshape(m, lru_w), y_fox.reshape(m, fox_w),
                            w_out.astype(BF16), h, "lru_fox_out_proj")


def _retention_layer(h, geom, norm, w_in, ret_norm, w_out):
    batch, rows, lead, _ = geom
    vw = ret_norm.shape[0]
    dk = (w_in.shape[1] - 2 * vw) // 2 // RET_HEADS
    xn = rmsnorm(h, norm)
    z = matmul(xn, w_in.astype(BF16), "retention_in_proj")
    half = dk // 2
    inv = ROPE_BASE ** (-jnp.arange(half, dtype=F32) / half)
    pos = (jnp.arange(rows) - lead).astype(F32)
    ang = pos[:, None] * inv[None, :]
    y = retention(z, jnp.cos(ang), jnp.sin(ang), ret_norm, batch=batch, rows=rows, lead=lead,
                  heads=RET_HEADS)
    return matmul_residual(y.reshape(batch * rows, vw), w_out.astype(BF16), h,
                           "retention_out_proj")


def kernel(x, meta_tokens, ab_norm, ab_w_in, ab_b_f, ab_conv_w, ab_conv_b, ab_w_a, ab_b_a, ab_w_x, ab_b_x, ab_lambda, ab_q_norm, ab_k_norm, ab_w_out, c_norm, c_w_in, c_ret_norm, c_w_out, ffn_norm, ffn_w_gate, ffn_w_up, ffn_w_down):
    batch, seq, d = x.shape
    n_meta = meta_tokens.shape[0]
    depth = ffn_norm.shape[0]
    lead = (-n_meta) % LANE
    head_rows = lead + n_meta
    rows = head_rows + seq
    assert seq % LANE == 0 and lead % RET_CHUNK == (-n_meta) % RET_CHUNK
    geom = (batch, rows, lead, head_rows)

    head = jnp.concatenate([jnp.zeros((lead, d), x.dtype), meta_tokens.astype(x.dtype)], axis=0)
    h = jnp.concatenate([jnp.broadcast_to(head[None], (batch, head_rows, d)), x], axis=1)
    h = h.reshape(batch * rows, d)
    for layer in range(depth):
        j = layer // 2
        if layer % 2 == 0:
            h = _lru_fox_layer(h, geom, ab_norm[j], ab_w_in[j], ab_b_f[j], ab_conv_w[j],
                               ab_conv_b[j], ab_w_a[j], ab_b_a[j], ab_w_x[j], ab_b_x[j],
                               ab_lambda[j], ab_q_norm[j], ab_k_norm[j], ab_w_out[j])
        else:
            h = _retention_layer(h, geom, c_norm[j], c_w_in[j], c_ret_norm[j], c_w_out[j])
        h = _ffn(h, ffn_norm[layer], ffn_w_gate[layer], ffn_w_up[layer], ffn_w_down[layer])
    return h.reshape(batch, rows, d)[:, head_rows:]
```

```python
import functools

import jax
import jax.numpy as jnp
from jax import lax
from jax.experimental import pallas as pl
from jax.experimental.pallas import tpu as pltpu

F32 = jnp.float32
BF16 = jnp.bfloat16

LANE = 128
SUBLANE = 8
VMEM_LIMIT_BYTES = 60 * 1024 * 1024

RMS_EPS = 1e-6
LRU_C = 8.0
RET_CHUNK = 64
RET_HEADS = 16
ROPE_BASE = 10000.0
MASK_VALUE = -1e30
LOG2E = 1.4426950408889634


def _divisor(n, candidates):
    for c in candidates:
        if n % c == 0:
            return c
    raise ValueError(f"no tile in {candidates} divides {n}")


def _params(*semantics):
    return pltpu.CompilerParams(dimension_semantics=semantics,
                                vmem_limit_bytes=VMEM_LIMIT_BYTES)


def _lane_partial_sums(sq):
    part = sq[:, :LANE]
    for c in range(1, sq.shape[1] // LANE):
        part = part + sq[:, c * LANE:(c + 1) * LANE]
    return part


def _embed_kernel(head_ref, x_ref, h_ref, hb_ref, ss_ref):
    t = pl.program_id(1)

    def emit(v):
        h_ref[0] = v
        hb_ref[0] = v.astype(BF16)
        ss_ref[0] = _lane_partial_sums(v * v)

    @pl.when(t == 0)
    def _():
        emit(head_ref[...])

    @pl.when(t > 0)
    def _():
        emit(x_ref[0])


def embed(x, head):
    batch, seq, d = x.shape
    hr = head.shape[0]
    rows = hr + seq
    nt = rows // hr
    blk = lambda b, t: (b, t, 0)
    return pl.pallas_call(
        _embed_kernel,
        out_shape=(jax.ShapeDtypeStruct((batch, rows, d), F32),
                   jax.ShapeDtypeStruct((batch, rows, d), BF16),
                   jax.ShapeDtypeStruct((batch, rows, LANE), F32)),
        grid=(batch, nt),
        in_specs=[pl.BlockSpec((hr, d), lambda b, t: (0, 0)),
                  pl.BlockSpec((1, hr, d), lambda b, t: (b, jnp.maximum(t - 1, 0), 0))],
        out_specs=(pl.BlockSpec((1, hr, d), blk), pl.BlockSpec((1, hr, d), blk),
                   pl.BlockSpec((1, hr, LANE), blk)),
        compiler_params=_params("parallel", "arbitrary"),
        name="embed",
    )(head, x)


def _row_scale(ss_ref, d):
    ms = jnp.sum(ss_ref[...], axis=-1, keepdims=True) * (1.0 / d)
    return jnp.broadcast_to(lax.rsqrt(ms + RMS_EPS), ss_ref.shape)


def _scaled(acc, rstd_ref):
    return acc * jnp.tile(rstd_ref[...], (1, acc.shape[1] // LANE))


def _norm_weight(w_ref, g_ref):
    return (w_ref[...] * g_ref[...]).astype(BF16)


_ROW_TILES = (1536, 768, 512, 256, 128, 8)


def _staged_maps(nj, n_steps):
    def product(t):
        u = jnp.maximum(t - 1, 0)
        return u // nj, u % nj
    return (lambda t: (product(t)[0], 0), lambda t: (0, product(t)[1]), product,
            lambda t: (0, jnp.minimum(t, n_steps - 2) % nj))


def _staged_step(t, nj, wb_ref, body):
    @pl.when(t == 0)
    def _():
        wb_ref[...] = jnp.zeros_like(wb_ref)

    @pl.when(t % 2 == 0)
    def _():
        body(0, 1)

    @pl.when(t % 2 == 1)
    def _():
        body(1, 0)


def _first_column(t, nj):
    return jnp.maximum(t - 1, 0) % nj == 0


def _norm_mm_kernel(x_ref, ss_ref, g_ref, w_ref, o_ref, rstd_ref, wb_ref, *, nj):
    t = pl.program_id(0)

    @pl.when(_first_column(t, nj))
    def _():
        rstd_ref[...] = _row_scale(ss_ref, x_ref.shape[1])

    def body(round_slot, use_slot):
        wb_ref[round_slot] = _norm_weight(w_ref, g_ref)
        acc = jnp.dot(x_ref[...], wb_ref[use_slot], preferred_element_type=F32)
        o_ref[...] = _scaled(acc, rstd_ref).astype(o_ref.dtype)

    _staged_step(t, nj, wb_ref, body)


def _norm_headnorm_mm_kernel(x_ref, ss_ref, g_ref, w_ref, hg_ref, o_ref, rstd_ref, wb_ref,
                             *, nj, norm_lo, norm_hi, head_dim):
    t = pl.program_id(0)
    j = jnp.maximum(t - 1, 0) % nj

    @pl.when(_first_column(t, nj))
    def _():
        rstd_ref[...] = _row_scale(ss_ref, x_ref.shape[1])

    def body(round_slot, use_slot):
        wb_ref[round_slot] = _norm_weight(w_ref, g_ref)
        acc = jnp.dot(x_ref[...], wb_ref[use_slot], preferred_element_type=F32)
        acc = _scaled(acc, rstd_ref)
        is_norm = jnp.logical_and(j >= norm_lo, j < norm_hi)

        @pl.when(is_norm)
        def _():
            for h in range(acc.shape[1] // head_dim):
                cols = slice(h * head_dim, (h + 1) * head_dim)
                y = acc[:, cols]
                ms = jnp.mean(y * y, axis=-1, keepdims=True)
                o_ref[:, cols] = (y * lax.rsqrt(ms + RMS_EPS) * hg_ref[:, cols]).astype(o_ref.dtype)

        @pl.when(jnp.logical_not(is_norm))
        def _():
            o_ref[...] = acc.astype(o_ref.dtype)

    _staged_step(t, nj, wb_ref, body)


def norm_matmul(hb, ss, gain, w, n_out, name, head_norm=None):
    m, k = hb.shape
    tm = _divisor(m, _ROW_TILES)
    tn = _divisor(n_out, (256, 128))
    nj = n_out // tn
    n_steps = (m // tm) * nj + 1
    row, col, out, wcol = _staged_maps(nj, n_steps)
    in_specs = [pl.BlockSpec((tm, k), row),
                pl.BlockSpec((tm, LANE), row),
                pl.BlockSpec((k, 1), lambda t: (0, 0)),
                pl.BlockSpec((k, tn), wcol)]
    args = [hb, ss, gain.reshape(k, 1).astype(F32), w]
    if head_norm is None:
        kern = functools.partial(_norm_mm_kernel, nj=nj)
    else:
        gains, (lo, hi), head_dim = head_norm
        assert tn % head_dim == 0 and lo % tn == 0 and hi % tn == 0
        kern = functools.partial(_norm_headnorm_mm_kernel, nj=nj, norm_lo=lo // tn,
                                 norm_hi=hi // tn, head_dim=head_dim)
        in_specs.append(pl.BlockSpec((1, tn), col))
        args.append(gains)
    return pl.pallas_call(
        kern,
        out_shape=jax.ShapeDtypeStruct((m, n_out), BF16),
        grid=(n_steps,),
        in_specs=in_specs,
        out_specs=pl.BlockSpec((tm, tn), out),
        scratch_shapes=[pltpu.VMEM((tm, LANE), F32), pltpu.VMEM((2, k, tn), BF16)],
        compiler_params=_params("arbitrary"),
        name=name,
    )(*args)


def _norm_swiglu_kernel(x_ref, ss_ref, g_ref, wg_ref, wu_ref, o_ref, rstd_ref, wb_ref, *, nj):
    t = pl.program_id(0)

    @pl.when(_first_column(t, nj))
    def _():
        rstd_ref[...] = _row_scale(ss_ref, x_ref.shape[1])

    def body(round_slot, use_slot):
        wb_ref[round_slot, 0] = _norm_weight(wg_ref, g_ref)
        wb_ref[round_slot, 1] = _norm_weight(wu_ref, g_ref)
        x = x_ref[...]
        gate = _scaled(jnp.dot(x, wb_ref[use_slot, 0], preferred_element_type=F32), rstd_ref)
        up = _scaled(jnp.dot(x, wb_ref[use_slot, 1], preferred_element_type=F32), rstd_ref)
        o_ref[...] = (gate * jax.nn.sigmoid(gate) * up).astype(o_ref.dtype)

    _staged_step(t, nj, wb_ref, body)


def norm_swiglu_up(hb, ss, gain, w_gate, w_up):
    m, k = hb.shape
    n = w_gate.shape[1]
    tm = _divisor(m, _ROW_TILES)
    tn = _divisor(n, (256, 128))
    nj = n // tn
    n_steps = (m // tm) * nj + 1
    row, _, out, wcol = _staged_maps(nj, n_steps)
    return pl.pallas_call(
        functools.partial(_norm_swiglu_kernel, nj=nj),
        out_shape=jax.ShapeDtypeStruct((m, n), BF16),
        grid=(n_steps,),
        in_specs=[pl.BlockSpec((tm, k), row),
                  pl.BlockSpec((tm, LANE), row),
                  pl.BlockSpec((k, 1), lambda t: (0, 0)),
                  pl.BlockSpec((k, tn), wcol),
                  pl.BlockSpec((k, tn), wcol)],
        out_specs=pl.BlockSpec((tm, tn), out),
        scratch_shapes=[pltpu.VMEM((tm, LANE), F32), pltpu.VMEM((2, 2, k, tn), BF16)],
        compiler_params=_params("arbitrary"),
        name="swiglu_up",
    )(hb, ss, gain.reshape(k, 1).astype(F32), w_gate, w_up)


def _emit_stream(h_new, first_column, h_ref, hb_ref, ss_ref):
    h_ref[...] = h_new
    hb_ref[...] = h_new.astype(BF16)
    part = _lane_partial_sums(h_new * h_new)

    @pl.when(first_column)
    def _():
        ss_ref[...] = part

    @pl.when(jnp.logical_not(first_column))
    def _():
        ss_ref[...] += part


def _residual_mm_kernel(x_ref, w_ref, r_ref, *out_refs):
    h_new = r_ref[...] + jnp.dot(x_ref[...], w_ref[...], preferred_element_type=F32)
    if len(out_refs) == 1:
        out_refs[0][...] = h_new
    else:
        _emit_stream(h_new, pl.program_id(1) == 0, *out_refs)


def residual_matmul(x, w, res, name, emit_norm=True):
    m, k = x.shape
    n = w.shape[1]
    tm = _divisor(m, _ROW_TILES[1:])
    tn = _divisor(n, (256, 128))
    tile = pl.BlockSpec((tm, tn), lambda i, j: (i, j))
    out_shape = [jax.ShapeDtypeStruct((m, n), F32)]
    out_specs = [tile]
    if emit_norm:
        out_shape += [jax.ShapeDtypeStruct((m, n), BF16), jax.ShapeDtypeStruct((m, LANE), F32)]
        out_specs += [tile, pl.BlockSpec((tm, LANE), lambda i, j: (i, 0))]
    return pl.pallas_call(
        _residual_mm_kernel,
        out_shape=tuple(out_shape),
        grid=(m // tm, n // tn),
        in_specs=[pl.BlockSpec((tm, k), lambda i, j: (i, 0)),
                  pl.BlockSpec((k, tn), lambda i, j: (0, j)),
                  tile],
        out_specs=tuple(out_specs),
        compiler_params=_params("parallel", "arbitrary"),
        name=name,
    )(x, w, res)


def _residual_mm2_kernel(xa_ref, xb_ref, wa_ref, wb_ref, r_ref, h_ref, hb_ref, ss_ref, wst_ref,
                         *, nj):
    t = pl.program_id(0)

    def body(round_slot, use_slot):
        wst_ref[round_slot, 0] = wa_ref[...].astype(BF16)
        wst_ref[round_slot, 1] = wb_ref[...].astype(BF16)
        acc = jnp.dot(xa_ref[...], wst_ref[use_slot, 0], preferred_element_type=F32)
        acc = acc + jnp.dot(xb_ref[...], wst_ref[use_slot, 1], preferred_element_type=F32)
        _emit_stream(r_ref[...] + acc, _first_column(t, nj), h_ref, hb_ref, ss_ref)

    _staged_step(t, nj, wst_ref, body)


def residual_matmul2(xa, xb, w, res, name):
    m, kx = xa.shape
    k, n = w.shape
    assert xb.shape == (m, kx) and k == 2 * kx
    tm = _divisor(m, _ROW_TILES)
    tn = _divisor(n, (256, 128))
    nj = n // tn
    n_steps = (m // tm) * nj + 1
    row, _, out, wcol = _staged_maps(nj, n_steps)
    tile = pl.BlockSpec((tm, tn), out)
    return pl.pallas_call(
        functools.partial(_residual_mm2_kernel, nj=nj),
        out_shape=(jax.ShapeDtypeStruct((m, n), F32), jax.ShapeDtypeStruct((m, n), BF16),
                   jax.ShapeDtypeStruct((m, LANE), F32)),
        grid=(n_steps,),
        in_specs=[pl.BlockSpec((tm, kx), row),
                  pl.BlockSpec((tm, kx), row),
                  pl.BlockSpec((kx, tn), wcol),
                  pl.BlockSpec((kx, tn), lambda t: (1, wcol(t)[1])),
                  tile],
        out_specs=(tile, tile, pl.BlockSpec((tm, LANE), row)),
        scratch_shapes=[pltpu.VMEM((2, 2, kx, tn), BF16)],
        compiler_params=_params("arbitrary"),
        name=name,
    )(xa, xb, w, w, res)


def _forget_logit_kernel(x_ref, ss_ref, g_ref, w_ref, b_ref, o_ref):
    acc = jnp.dot(x_ref[...], _norm_weight(w_ref, g_ref), preferred_element_type=F32)
    f = acc * _row_scale(ss_ref, x_ref.shape[1]) + b_ref[...]
    o_ref[...] = jnp.minimum(f, 0.0) - jnp.log1p(jnp.exp(-jnp.abs(f)))


def forget_log_gates(hb, ss, gain, w_f, b_f):
    m, k = hb.shape
    n = w_f.shape[1]
    assert n == LANE
    tm = _divisor(m, _ROW_TILES)
    return pl.pallas_call(
        _forget_logit_kernel,
        out_shape=jax.ShapeDtypeStruct((m, n), F32),
        grid=(m // tm,),
        in_specs=[pl.BlockSpec((tm, k), lambda i: (i, 0)),
                  pl.BlockSpec((tm, LANE), lambda i: (i, 0)),
                  pl.BlockSpec((k, 1), lambda i: (0, 0)),
                  pl.BlockSpec((k, n), lambda i: (0, 0)),
                  pl.BlockSpec((1, n), lambda i: (0, 0))],
        out_specs=pl.BlockSpec((tm, n), lambda i: (i, 0)),
        compiler_params=_params("parallel"),
        name="fox_forget_gates",
    )(hb, ss, gain.reshape(k, 1).astype(F32), w_f, b_f)


def _prefix_sum_kernel(x_ref, o_ref, *, lead):
    x = x_ref[0]
    rows = x.shape[0]
    row = lax.broadcasted_iota(jnp.int32, x.shape, 0)
    x = jnp.where(row >= lead, x, 0.0)
    shift = 1
    while shift < rows:
        x = x + jnp.where(row >= shift, pltpu.roll(x, shift, 0), 0.0)
        shift *= 2
    o_ref[0] = x


def prefix_sum_time(x, lead):
    b, rows, lanes = x.shape
    return pl.pallas_call(
        functools.partial(_prefix_sum_kernel, lead=lead),
        out_shape=jax.ShapeDtypeStruct(x.shape, F32),
        grid=(b,),
        in_specs=[pl.BlockSpec((1, rows, lanes), lambda i: (i, 0, 0))],
        out_specs=pl.BlockSpec((1, rows, lanes), lambda i: (i, 0, 0)),
        compiler_params=_params("parallel"),
        name="fox_prefix_sum",
    )(x)


def _split3(x):
    hi = x.astype(BF16).astype(F32)
    rem = x - hi
    mid = rem.astype(BF16).astype(F32)
    lo = (rem - mid).astype(BF16).astype(F32)
    return hi, mid, lo


def _fox_kernel(q_ref, k_ref, v_ref, cum_ref, o_ref, qb_ref, kb_ref, m_ref, acc_ref,
                *, lead, head_rows, tq):
    h = pl.program_id(1)
    rows, dh = q_ref.shape[1], q_ref.shape[2]
    n_main = (rows - head_rows) // tq
    lanes = cum_ref.shape[2]

    def build_bias(r0, n):
        lane = lax.broadcasted_iota(jnp.int32, (n, lanes), 1)
        c = jnp.sum(jnp.where(lane == h, cum_ref[0, pl.ds(r0, n), :], 0.0),
                    axis=-1, keepdims=True) * LOG2E
        hi, mid, lo = _split3(c)
        is_pad = (r0 + lax.broadcasted_iota(jnp.int32, (n, 1), 0)) < lead
        qb = jnp.where(lane == 0, hi, jnp.where(lane == 1, mid, jnp.where(lane == 2, lo,
             jnp.where(lane < 7, 1.0, 0.0))))
        kb = jnp.where(lane < 3, 1.0, jnp.where(lane == 3, -hi, jnp.where(lane == 4, -mid,
             jnp.where(lane == 5, -lo, jnp.where(jnp.logical_and(lane == 6, is_pad),
                                                  MASK_VALUE, 0.0)))))
        qb_ref[pl.ds(r0, n), :] = qb.astype(BF16)
        kb_ref[pl.ds(r0, n), :] = kb.astype(BF16)

    build_bias(0, head_rows)
    for i in range(n_main):
        build_bias(head_rows + i * tq, tq)

    def row_max(s):
        return jnp.broadcast_to(jnp.max(s, axis=-1, keepdims=True), (s.shape[0], lanes))

    def scores(q_aug, k0, nk):
        k_aug = jnp.concatenate([k_ref[0, pl.ds(k0, nk), :], kb_ref[pl.ds(k0, nk), :]], axis=1)
        return lax.dot_general(q_aug, k_aug, (((1,), (1,)), ((), ())),
                               preferred_element_type=F32)

    def values(k0, nk):
        return jnp.concatenate([v_ref[0, pl.ds(k0, nk), :], jnp.ones((nk, dh), BF16)], axis=1)

    def accumulate(s, s_max, v_aug):
        nq, nk = s.shape
        m_prev = m_ref[pl.ds(0, nq), :]
        m_new = jnp.maximum(m_prev, s_max)
        alpha = jnp.exp2(m_prev - m_new)
        p = jnp.exp2(s - jnp.tile(m_new, (1, nk // lanes)))
        acc_ref[pl.ds(0, nq), :] = (jnp.tile(alpha, (1, 2)) * acc_ref[pl.ds(0, nq), :]
                                    + jnp.dot(p.astype(BF16), v_aug, preferred_element_type=F32))
        m_ref[pl.ds(0, nq), :] = m_new

    def causal(s):
        return jnp.where(lax.broadcasted_iota(jnp.int32, s.shape, 1)
                         <= lax.broadcasted_iota(jnp.int32, s.shape, 0), s, MASK_VALUE)

    def reset(nq):
        m_ref[pl.ds(0, nq), :] = jnp.full((nq, lanes), MASK_VALUE, F32)
        acc_ref[pl.ds(0, nq), :] = jnp.zeros((nq, 2 * dh), F32)

    def finish(r0, nq):
        o_ref[0, pl.ds(r0, nq), :] = (acc_ref[pl.ds(0, nq), pl.ds(0, dh)]
                                      / acc_ref[pl.ds(0, nq), pl.ds(dh, dh)]).astype(o_ref.dtype)

    def q_rows(r0, nq):
        return jnp.concatenate([q_ref[0, pl.ds(r0, nq), :], qb_ref[pl.ds(r0, nq), :]], axis=1)

    reset(head_rows)
    s_head = causal(scores(q_rows(0, head_rows), 0, head_rows))
    accumulate(s_head, row_max(s_head), values(0, head_rows))
    finish(0, head_rows)

    def q_tile(i, carry):
        r0 = pl.multiple_of(head_rows + i * tq, LANE)
        q_aug = q_rows(r0, tq)
        reset(tq)

        def kv_tile(j, s):
            k0 = pl.multiple_of(head_rows + j * tq, LANE)
            s_next = scores(q_aug, pl.multiple_of(k0 + tq, LANE), tq)
            accumulate(s, row_max(s), values(k0, tq))
            return s_next

        s_diag = lax.fori_loop(0, i, kv_tile, scores(q_aug, head_rows, tq))
        s_last = jnp.concatenate([causal(s_diag), scores(q_aug, 0, head_rows)], axis=1)
        v_last = jnp.concatenate([values(r0, tq), values(0, head_rows)], axis=0)
        accumulate(s_last, row_max(s_last), v_last)
        finish(r0, tq)
        return carry

    lax.fori_loop(0, n_main, q_tile, 0)


def fox_attention(z, cum, *, batch, rows, lead, head_rows, heads, head_dim, q_col, k_col, v_col):
    z3 = z.reshape(batch, rows, z.shape[1])
    lanes = cum.shape[2]
    assert head_dim == lanes and head_rows == lanes
    tq = _divisor(rows - head_rows, (512, 256, 128))
    qb, kb, vb = q_col // head_dim, k_col // head_dim, v_col // head_dim
    kern = functools.partial(_fox_kernel, lead=lead, head_rows=head_rows, tq=tq)
    tmax = max(tq, head_rows)
    return pl.pallas_call(
        kern,
        out_shape=jax.ShapeDtypeStruct((batch, rows, heads * head_dim), BF16),
        grid=(batch, heads),
        in_specs=[pl.BlockSpec((1, rows, head_dim), lambda b, h: (b, 0, qb + h)),
                  pl.BlockSpec((1, rows, head_dim), lambda b, h: (b, 0, kb + h)),
                  pl.BlockSpec((1, rows, head_dim), lambda b, h: (b, 0, vb + h)),
                  pl.BlockSpec((1, rows, lanes), lambda b, h: (b, 0, 0))],
        out_specs=pl.BlockSpec((1, rows, head_dim), lambda b, h: (b, 0, h)),
        scratch_shapes=[pltpu.VMEM((rows, lanes), BF16), pltpu.VMEM((rows, lanes), BF16),
                        pltpu.VMEM((tmax, lanes), F32), pltpu.VMEM((tmax, 2 * head_dim), F32)],
        compiler_params=_params("parallel", "arbitrary"),
        name="fox_attention",
    )(z3, z3, z3, cum)


def _lru_kernel(x_ref, gate_ref, cw_ref, cb_ref, wa_ref, ba_ref, wx_ref, bx_ref, lam_ref,
                o_ref, ext_ref, a_ref, b_ref, carry_ref, *, lead, block_dim):
    t = pl.program_id(2)
    tl, tc = x_ref.shape[1], x_ref.shape[2]
    taps = cw_ref.shape[0]

    @pl.when(t == 0)
    def _():
        ext_ref[pl.ds(0, SUBLANE), :] = jnp.zeros((SUBLANE, tc), F32)
        carry_ref[...] = jnp.zeros_like(carry_ref)

    row = t * tl + lax.broadcasted_iota(jnp.int32, (tl, 1), 0)
    valid = row >= lead
    x = jnp.where(valid, x_ref[0].astype(F32), 0.0)
    ext_ref[pl.ds(SUBLANE, tl), :] = x

    xc = cb_ref[...] + ext_ref[pl.ds(SUBLANE - (taps - 1), tl), :] * cw_ref[pl.ds(0, 1), :]
    for j in range(1, taps):
        xc = xc + ext_ref[pl.ds(SUBLANE - (taps - 1) + j, tl), :] * cw_ref[pl.ds(j, 1), :]
    ext_ref[pl.ds(0, SUBLANE), :] = ext_ref[pl.ds(tl, SUBLANE), :]

    xcb = xc.astype(BF16)
    lam = lam_ref[...]
    softplus_neg_lam = jnp.maximum(-lam, 0.0) + jnp.log1p(jnp.exp(-jnp.abs(lam)))
    rate = -LRU_C * softplus_neg_lam
    for g in range(tc // block_dim):
        cols = slice(g * block_dim, (g + 1) * block_dim)
        xg = xcb[:, cols]
        r = jax.nn.sigmoid(jnp.dot(xg, wa_ref[g], preferred_element_type=F32) + ba_ref[:, cols])
        i = jax.nn.sigmoid(jnp.dot(xg, wx_ref[g], preferred_element_type=F32) + bx_ref[:, cols])
        log_a = rate[:, cols] * r
        a = jnp.exp(log_a)
        gated = jnp.sqrt(1.0 - jnp.exp(2.0 * log_a)) * (i * xc[:, cols])
        a_ref[:, cols] = a
        b_ref[:, cols] = jnp.where(valid, gated, 0.0)

    a = a_ref[...]
    b = b_ref[...]
    sub = lax.broadcasted_iota(jnp.int32, (tl, 1), 0) % SUBLANE
    shift = 1
    while shift < SUBLANE:
        keep = sub >= shift
        a_prev = jnp.where(keep, pltpu.roll(a, shift, 0), 1.0)
        b_prev = jnp.where(keep, pltpu.roll(b, shift, 0), 0.0)
        b = a * b_prev + b
        a = a * a_prev
        shift *= 2
    a_ref[...] = a
    b_ref[...] = b

    def group(gidx, h_prev):
        r0 = pl.multiple_of(gidx * SUBLANE, SUBLANE)
        hg = a_ref[pl.ds(r0, SUBLANE), :] * h_prev + b_ref[pl.ds(r0, SUBLANE), :]
        b_ref[pl.ds(r0, SUBLANE), :] = hg
        return hg[SUBLANE - 1:SUBLANE, :]

    carry_ref[...] = lax.fori_loop(0, tl // SUBLANE, group, carry_ref[...], unroll=4)

    gate = gate_ref[0].astype(F32)
    o_ref[0] = (b_ref[...] * jax.nn.gelu(gate)).astype(o_ref.dtype)


def lru_branch(z, conv_w, conv_b, w_a, b_a, w_x, b_x, lam, *, batch, rows, lead, width):
    z3 = z.reshape(batch, rows, z.shape[1])
    nblk, bd, _ = w_a.shape
    tc = _divisor(width, (512, 256, 128))
    tl = _divisor(rows, (384, 256, 128, 64, 8))
    nct = width // tc
    row2 = lambda v: v.reshape(1, width).astype(F32)
    kern = functools.partial(_lru_kernel, lead=lead, block_dim=bd)
    vec = pl.BlockSpec((1, tc), lambda b, c, t: (0, c))
    return pl.pallas_call(
        kern,
        out_shape=jax.ShapeDtypeStruct((batch, rows, width), BF16),
        grid=(batch, nct, rows // tl),
        in_specs=[pl.BlockSpec((1, tl, tc), lambda b, c, t: (b, t, c)),
                  pl.BlockSpec((1, tl, tc), lambda b, c, t: (b, t, nct + c)),
                  pl.BlockSpec((conv_w.shape[0], tc), lambda b, c, t: (0, c)),
                  vec,
                  pl.BlockSpec((tc // bd, bd, bd), lambda b, c, t: (c, 0, 0)),
                  vec,
                  pl.BlockSpec((tc // bd, bd, bd), lambda b, c, t: (c, 0, 0)),
                  vec, vec],
        out_specs=pl.BlockSpec((1, tl, tc), lambda b, c, t: (b, t, c)),
        scratch_shapes=[pltpu.VMEM((tl + SUBLANE, tc), F32), pltpu.VMEM((tl, tc), F32),
                        pltpu.VMEM((tl, tc), F32), pltpu.VMEM((1, tc), F32)],
        compiler_params=_params("parallel", "parallel", "arbitrary"),
        name="rg_lru",
    )(z3, z3, conv_w.astype(F32), row2(conv_b), w_a.astype(BF16), row2(b_a),
      w_x.astype(BF16), row2(b_x), row2(lam))


def _retention_kernel(lg_ref, q_ref, k_ref, v_ref, g_ref, cos_ref, sin_ref, gain_ref, o_ref,
                      s_ref, d_ref, *, lead, tb, chunk):
    h = pl.program_id(1)
    rows = q_ref.shape[1]
    dk = q_ref.shape[2]
    half = dk // 2
    lg = lg_ref[h]

    ti = lax.broadcasted_iota(jnp.int32, (tb, tb), 0)
    si = lax.broadcasted_iota(jnp.int32, (tb, tb), 1)
    dist = jnp.abs(ti - si).astype(F32)
    d_ref[...] = jnp.where(si // chunk <= ti // chunk, jnp.exp(lg * dist), 0.0)
    idx = lax.broadcasted_iota(jnp.int32, (tb, 1), 0).astype(F32)
    q_decay = jnp.exp(lg * (idx + 1.0))
    k_decay = jnp.exp(lg * (tb - 1.0 - idx))
    block_decay = jnp.exp(lg * jnp.full((1, 1), float(tb), F32))
    s_ref[...] = jnp.zeros_like(s_ref)

    def rotate(x, c, s):
        x1, x2 = x[:, :half], x[:, half:]
        return jnp.concatenate([x1 * c - x2 * s, x1 * s + x2 * c], axis=1)

    def block(n, carry):
        r0 = pl.multiple_of(n * tb, LANE)
        valid = (r0 + lax.broadcasted_iota(jnp.int32, (tb, 1), 0)) >= lead
        c = cos_ref[pl.ds(r0, tb), :]
        s = sin_ref[pl.ds(r0, tb), :]
        q = rotate(q_ref[0, pl.ds(r0, tb), :].astype(F32), c, s)
        k = rotate(k_ref[0, pl.ds(r0, tb), :].astype(F32), c, s) * (dk ** -0.5)
        k = jnp.where(valid, k, 0.0)
        v = v_ref[0, pl.ds(r0, tb), :]
        v = jnp.where(valid, v, jnp.zeros_like(v))

        w = lax.dot_general(q.astype(BF16), k.astype(BF16), (((1,), (1,)), ((), ())),
                            preferred_element_type=F32) * d_ref[...]
        state = s_ref[...]
        o = jnp.dot(w.astype(BF16), v, preferred_element_type=F32)
        o = o + jnp.dot((q * q_decay).astype(BF16), state.astype(BF16),
                        preferred_element_type=F32)
        s_ref[...] = state * block_decay + lax.dot_general(
            (k * k_decay).astype(BF16), v, (((0,), (0,)), ((), ())),
            preferred_element_type=F32)

        ms = jnp.mean(o * o, axis=-1, keepdims=True)
        on = o * lax.rsqrt(ms + RMS_EPS) * gain_ref[0]
        g = g_ref[0, pl.ds(r0, tb), :].astype(F32)
        o_ref[0, pl.ds(r0, tb), :] = (g * jax.nn.sigmoid(g) * on).astype(o_ref.dtype)
        return carry

    lax.fori_loop(0, rows // tb, block, 0)


def retention(z, cos, sin, ret_norm, *, batch, rows, lead, heads):
    cols = z.shape[1]
    z3 = z.reshape(batch, rows, cols)
    vw = ret_norm.shape[0]
    qk = (cols - 2 * vw) // 2
    dk, dv = qk // heads, vw // heads
    tb = _divisor(rows, (384, 128, 64))
    assert tb % RET_CHUNK == 0
    log_gamma = jnp.log(1.0 - 2.0 ** (-5.0 - jnp.arange(heads, dtype=F32)))
    kern = functools.partial(_retention_kernel, lead=lead, tb=tb, chunk=RET_CHUNK)
    kb, vb, gb = qk // dk, 2 * qk // dv, (2 * qk + vw) // dv
    return pl.pallas_call(
        kern,
        out_shape=jax.ShapeDtypeStruct((batch, rows, vw), BF16),
        grid=(batch, heads),
        in_specs=[pl.BlockSpec(memory_space=pltpu.SMEM),
                  pl.BlockSpec((1, rows, dk), lambda b, h: (b, 0, h)),
                  pl.BlockSpec((1, rows, dk), lambda b, h: (b, 0, kb + h)),
                  pl.BlockSpec((1, rows, dv), lambda b, h: (b, 0, vb + h)),
                  pl.BlockSpec((1, rows, dv), lambda b, h: (b, 0, gb + h)),
                  pl.BlockSpec((rows, dk // 2), lambda b, h: (0, 0)),
                  pl.BlockSpec((rows, dk // 2), lambda b, h: (0, 0)),
                  pl.BlockSpec((1, 1, dv), lambda b, h: (h, 0, 0))],
        out_specs=pl.BlockSpec((1, rows, dv), lambda b, h: (b, 0, h)),
        scratch_shapes=[pltpu.VMEM((dk, dv), F32), pltpu.VMEM((tb, tb), F32)],
        compiler_params=_params("parallel", "arbitrary"),
        name="retention",
    )(log_gamma, z3, z3, z3, z3, cos, sin, ret_norm.reshape(heads, 1, dv).astype(F32))


def _ffn(stream, norm, w_gate, w_up, w_down, last):
    h, hb, ss = stream
    a = norm_swiglu_up(hb, ss, norm, w_gate, w_up)
    return residual_matmul(a, w_down.astype(BF16), h, "ffn_down", emit_norm=not last)


def _lru_fox_layer(stream, geom, norm, w_in, b_f, conv_w, conv_b, w_a, b_a, w_x, b_x, lam,
                   q_norm, k_norm, w_out):
    h, hb, ss = stream
    batch, rows, lead, head_rows = geom
    heads = b_f.shape[0]
    head_dim = q_norm.shape[0]
    fox_w = heads * head_dim
    lru_w = lam.shape[0]
    q_col, k_col, v_col, f_col = 2 * lru_w, 2 * lru_w + fox_w, 2 * lru_w + 2 * fox_w, 2 * lru_w + 3 * fox_w

    gains = jnp.ones((1, f_col), F32)
    gains = gains.at[0, q_col:k_col].set(
        jnp.tile(q_norm.astype(F32) * (head_dim ** -0.5 * LOG2E), heads))
    gains = gains.at[0, k_col:v_col].set(jnp.tile(k_norm.astype(F32), heads))
    z = norm_matmul(hb, ss, norm, w_in, f_col, "lru_fox_in_proj",
                    head_norm=(gains, (q_col, v_col), head_dim))

    w_f = jnp.zeros((w_in.shape[0], LANE), F32).at[:, :heads].set(w_in[:, f_col:])
    bias_f = jnp.zeros((1, LANE), F32).at[0, :heads].set(b_f.astype(F32))
    log_f = forget_log_gates(hb, ss, norm, w_f, bias_f).reshape(batch, rows, LANE)
    cum = prefix_sum_time(log_f, lead)

    y_lru = lru_branch(z, conv_w, conv_b, w_a, b_a, w_x, b_x, lam,
                       batch=batch, rows=rows, lead=lead, width=lru_w)
    y_fox = fox_attention(z, cum, batch=batch, rows=rows, lead=lead, head_rows=head_rows,
                          heads=heads, head_dim=head_dim, q_col=q_col, k_col=k_col, v_col=v_col)
    m = batch * rows
    return residual_matmul2(y_lru.reshape(m, lru_w), y_fox.reshape(m, fox_w), w_out, h,
                            "lru_fox_out_proj")


def _retention_layer(stream, geom, norm, w_in, ret_norm, w_out):
    h, hb, ss = stream
    batch, rows, lead, _ = geom
    vw = ret_norm.shape[0]
    dk = (w_in.shape[1] - 2 * vw) // 2 // RET_HEADS
    z = norm_matmul(hb, ss, norm, w_in, w_in.shape[1], "retention_in_proj")
    half = dk // 2
    inv = ROPE_BASE ** (-jnp.arange(half, dtype=F32) / half)
    pos = (jnp.arange(rows) - lead).astype(F32)
    ang = pos[:, None] * inv[None, :]
    y = retention(z, jnp.cos(ang), jnp.sin(ang), ret_norm, batch=batch, rows=rows, lead=lead,
                  heads=RET_HEADS)
    return residual_matmul(y.reshape(batch * rows, vw), w_out.astype(BF16), h,
                           "retention_out_proj")


def kernel(x, meta_tokens, ab_norm, ab_w_in, ab_b_f, ab_conv_w, ab_conv_b, ab_w_a, ab_b_a, ab_w_x, ab_b_x, ab_lambda, ab_q_norm, ab_k_norm, ab_w_out, c_norm, c_w_in, c_ret_norm, c_w_out, ffn_norm, ffn_w_gate, ffn_w_up, ffn_w_down):
    batch, seq, d = x.shape
    n_meta = meta_tokens.shape[0]
    depth = ffn_norm.shape[0]
    lead = (-n_meta) % LANE
    head_rows = lead + n_meta
    rows = head_rows + seq
    assert seq % LANE == 0 and lead % RET_CHUNK == (-n_meta) % RET_CHUNK
    geom = (batch, rows, lead, head_rows)
    m = batch * rows

    head = jnp.concatenate([jnp.zeros((lead, d), F32), meta_tokens.astype(F32)], axis=0)
    stream = tuple(a.reshape(m, a.shape[-1]) for a in embed(x.astype(F32), head))
    for layer in range(depth):
        j = layer // 2
        if layer % 2 == 0:
            stream = _lru_fox_layer(stream, geom, ab_norm[j], ab_w_in[j], ab_b_f[j], ab_conv_w[j],
                                    ab_conv_b[j], ab_w_a[j], ab_b_a[j], ab_w_x[j], ab_b_x[j],
                                    ab_lambda[j], ab_q_norm[j], ab_k_norm[j], ab_w_out[j])
        else:
            stream = _retention_layer(stream, geom, c_norm[j], c_w_in[j], c_ret_norm[j], c_w_out[j])
        stream = _ffn(stream, ffn_norm[layer], ffn_w_gate[layer], ffn_w_up[layer],
                      ffn_w_down[layer], last=layer == depth - 1)
    return stream[0].reshape(batch, rows, d)[:, head_rows:]
```

```python
import functools

import jax
import jax.numpy as jnp
from jax import lax
from jax.experimental import pallas as pl
from jax.experimental.pallas import tpu as pltpu

F32 = jnp.float32
BF16 = jnp.bfloat16

LANE = 128
SUBLANE = 8
VMEM_LIMIT_BYTES = 60 * 1024 * 1024

RMS_EPS = 1e-6
LRU_C = 8.0
RET_CHUNK = 64
RET_HEADS = 16
ROPE_BASE = 10000.0
MASK_VALUE = -1e30
LOG2E = 1.4426950408889634


def _divisor(n, candidates):
    for c in candidates:
        if n % c == 0:
            return c
    raise ValueError(f"no tile in {candidates} divides {n}")


def _params(*semantics):
    return pltpu.CompilerParams(dimension_semantics=semantics,
                                vmem_limit_bytes=VMEM_LIMIT_BYTES)


def _lane_partial_sums(sq):
    part = sq[:, :LANE]
    for c in range(1, sq.shape[1] // LANE):
        part = part + sq[:, c * LANE:(c + 1) * LANE]
    return part


def _embed_kernel(head_ref, x_ref, h_ref, hb_ref, ss_ref):
    t = pl.program_id(1)

    def emit(v):
        h_ref[0] = v
        hb_ref[0] = v.astype(BF16)
        ss_ref[0] = _lane_partial_sums(v * v)

    @pl.when(t == 0)
    def _():
        emit(head_ref[...])

    @pl.when(t > 0)
    def _():
        emit(x_ref[0])


def embed(x, head):
    batch, seq, d = x.shape
    hr = head.shape[0]
    rows = hr + seq
    nt = rows // hr
    blk = lambda b, t: (b, t, 0)
    return pl.pallas_call(
        _embed_kernel,
        out_shape=(jax.ShapeDtypeStruct((batch, rows, d), F32),
                   jax.ShapeDtypeStruct((batch, rows, d), BF16),
                   jax.ShapeDtypeStruct((batch, rows, LANE), F32)),
        grid=(batch, nt),
        in_specs=[pl.BlockSpec((hr, d), lambda b, t: (0, 0)),
                  pl.BlockSpec((1, hr, d), lambda b, t: (b, jnp.maximum(t - 1, 0), 0))],
        out_specs=(pl.BlockSpec((1, hr, d), blk), pl.BlockSpec((1, hr, d), blk),
                   pl.BlockSpec((1, hr, LANE), blk)),
        compiler_params=_params("parallel", "arbitrary"),
        name="embed",
    )(head, x)


def _row_scale(ss_ref, d):
    ms = jnp.sum(ss_ref[...], axis=-1, keepdims=True) * (1.0 / d)
    return jnp.broadcast_to(lax.rsqrt(ms + RMS_EPS), ss_ref.shape)


def _scaled(acc, rstd_ref):
    return acc * jnp.tile(rstd_ref[...], (1, acc.shape[1] // LANE))


def _norm_weight(w_ref, g_ref):
    return (w_ref[...] * g_ref[...]).astype(BF16)


_ROW_TILES = (1536, 768, 512, 256, 128, 8)


def _staged_maps(nj, n_steps):
    def product(t):
        u = jnp.maximum(t - 1, 0)
        return u // nj, u % nj
    return (lambda t: (product(t)[0], 0), lambda t: (0, product(t)[1]), product,
            lambda t: (0, jnp.minimum(t, n_steps - 2) % nj))


def _staged_step(t, nj, wb_ref, body):
    @pl.when(t == 0)
    def _():
        wb_ref[...] = jnp.zeros_like(wb_ref)

    @pl.when(t % 2 == 0)
    def _():
        body(0, 1)

    @pl.when(t % 2 == 1)
    def _():
        body(1, 0)


def _first_column(t, nj):
    return jnp.maximum(t - 1, 0) % nj == 0


def _norm_mm_kernel(x_ref, ss_ref, g_ref, w_ref, o_ref, rstd_ref, wb_ref, *, nj):
    t = pl.program_id(0)

    @pl.when(_first_column(t, nj))
    def _():
        rstd_ref[...] = _row_scale(ss_ref, x_ref.shape[1])

    def body(round_slot, use_slot):
        wb_ref[round_slot] = _norm_weight(w_ref, g_ref)
        acc = jnp.dot(x_ref[...], wb_ref[use_slot], preferred_element_type=F32)
        o_ref[...] = _scaled(acc, rstd_ref).astype(o_ref.dtype)

    _staged_step(t, nj, wb_ref, body)


def _norm_headnorm_mm_kernel(x_ref, ss_ref, g_ref, w_ref, hg_ref, o_ref, rstd_ref, wb_ref,
                             *, nj, norm_lo, norm_hi, head_dim):
    t = pl.program_id(0)
    j = jnp.maximum(t - 1, 0) % nj

    @pl.when(_first_column(t, nj))
    def _():
        rstd_ref[...] = _row_scale(ss_ref, x_ref.shape[1])

    def body(round_slot, use_slot):
        wb_ref[round_slot] = _norm_weight(w_ref, g_ref)
        acc = jnp.dot(x_ref[...], wb_ref[use_slot], preferred_element_type=F32)
        acc = _scaled(acc, rstd_ref)
        is_norm = jnp.logical_and(j >= norm_lo, j < norm_hi)

        @pl.when(is_norm)
        def _():
            for h in range(acc.shape[1] // head_dim):
                cols = slice(h * head_dim, (h + 1) * head_dim)
                y = acc[:, cols]
                ms = jnp.mean(y * y, axis=-1, keepdims=True)
                o_ref[:, cols] = (y * lax.rsqrt(ms + RMS_EPS) * hg_ref[:, cols]).astype(o_ref.dtype)

        @pl.when(jnp.logical_not(is_norm))
        def _():
            o_ref[...] = acc.astype(o_ref.dtype)

    _staged_step(t, nj, wb_ref, body)


def _stacked(w, layer, block, index_map):
    return pl.BlockSpec((None,) + block, lambda *idx: (layer,) + tuple(index_map(*idx)))


def norm_matmul(hb, ss, gain, w, layer, n_out, name, head_norm=None):
    m, k = hb.shape
    tm = _divisor(m, _ROW_TILES)
    tn = _divisor(n_out, (512, 256, 128))
    nj = n_out // tn
    n_steps = (m // tm) * nj + 1
    row, col, out, wcol = _staged_maps(nj, n_steps)
    in_specs = [pl.BlockSpec((tm, k), row, pipeline_mode=pl.Buffered(1)),
                pl.BlockSpec((tm, LANE), row),
                pl.BlockSpec((k, 1), lambda t: (0, 0)),
                _stacked(w, layer, (k, tn), wcol)]
    args = [hb, ss, gain.reshape(k, 1).astype(F32), w]
    if head_norm is None:
        kern = functools.partial(_norm_mm_kernel, nj=nj)
    else:
        gains, (lo, hi), head_dim = head_norm
        assert tn % head_dim == 0 and lo % tn == 0 and hi % tn == 0
        kern = functools.partial(_norm_headnorm_mm_kernel, nj=nj, norm_lo=lo // tn,
                                 norm_hi=hi // tn, head_dim=head_dim)
        in_specs.append(pl.BlockSpec((1, tn), col))
        args.append(gains)
    return pl.pallas_call(
        kern,
        out_shape=jax.ShapeDtypeStruct((m, n_out), BF16),
        grid=(n_steps,),
        in_specs=in_specs,
        out_specs=pl.BlockSpec((tm, tn), out),
        scratch_shapes=[pltpu.VMEM((tm, LANE), F32), pltpu.VMEM((2, k, tn), BF16)],
        compiler_params=_params("arbitrary"),
        name=name,
    )(*args)


def _norm_swiglu_kernel(x_ref, ss_ref, g_ref, wg_ref, wu_ref, o_ref, rstd_ref, wb_ref, *, nj):
    t = pl.program_id(0)

    @pl.when(_first_column(t, nj))
    def _():
        rstd_ref[...] = _row_scale(ss_ref, x_ref.shape[1])

    def body(round_slot, use_slot):
        wb_ref[round_slot, 0] = _norm_weight(wg_ref, g_ref)
        wb_ref[round_slot, 1] = _norm_weight(wu_ref, g_ref)
        x = x_ref[...]
        gate = _scaled(jnp.dot(x, wb_ref[use_slot, 0], preferred_element_type=F32), rstd_ref)
        up = _scaled(jnp.dot(x, wb_ref[use_slot, 1], preferred_element_type=F32), rstd_ref)
        o_ref[...] = (gate * jax.nn.sigmoid(gate) * up).astype(o_ref.dtype)

    _staged_step(t, nj, wb_ref, body)


def norm_swiglu_up(hb, ss, gain, w_gate, w_up, layer):
    m, k = hb.shape
    n = w_gate.shape[2]
    tm = _divisor(m, _ROW_TILES)
    tn = _divisor(n, (256, 128))
    nj = n // tn
    n_steps = (m // tm) * nj + 1
    row, _, out, wcol = _staged_maps(nj, n_steps)
    return pl.pallas_call(
        functools.partial(_norm_swiglu_kernel, nj=nj),
        out_shape=jax.ShapeDtypeStruct((m, n), BF16),
        grid=(n_steps,),
        in_specs=[pl.BlockSpec((tm, k), row),
                  pl.BlockSpec((tm, LANE), row),
                  pl.BlockSpec((k, 1), lambda t: (0, 0)),
                  _stacked(w_gate, layer, (k, tn), wcol),
                  _stacked(w_up, layer, (k, tn), wcol)],
        out_specs=pl.BlockSpec((tm, tn), out),
        scratch_shapes=[pltpu.VMEM((tm, LANE), F32), pltpu.VMEM((2, 2, k, tn), BF16)],
        compiler_params=_params("arbitrary"),
        name="swiglu_up",
    )(hb, ss, gain.reshape(k, 1).astype(F32), w_gate, w_up)


def _emit_stream(h_new, first_column, h_ref, hb_ref, ss_ref):
    h_ref[...] = h_new
    hb_ref[...] = h_new.astype(BF16)
    part = _lane_partial_sums(h_new * h_new)

    @pl.when(first_column)
    def _():
        ss_ref[...] = part

    @pl.when(jnp.logical_not(first_column))
    def _():
        ss_ref[...] += part


def _residual_mm_kernel(x_ref, w_ref, r_ref, *out_refs):
    h_new = r_ref[...] + jnp.dot(x_ref[...], w_ref[...], preferred_element_type=F32)
    if len(out_refs) == 1:
        out_refs[0][...] = h_new
    else:
        _emit_stream(h_new, pl.program_id(1) == 0, *out_refs)


def residual_matmul(x, w, layer, res, name, emit_norm=True):
    m, k = x.shape
    n = w.shape[2]
    tm = _divisor(m, _ROW_TILES[1:])
    tn = _divisor(n, (256, 128))
    tile = pl.BlockSpec((tm, tn), lambda i, j: (i, j))
    out_shape = [jax.ShapeDtypeStruct((m, n), F32)]
    out_specs = [tile]
    if emit_norm:
        out_shape += [jax.ShapeDtypeStruct((m, n), BF16), jax.ShapeDtypeStruct((m, LANE), F32)]
        out_specs += [tile, pl.BlockSpec((tm, LANE), lambda i, j: (i, 0))]
    return pl.pallas_call(
        _residual_mm_kernel,
        out_shape=tuple(out_shape),
        grid=(m // tm, n // tn),
        in_specs=[pl.BlockSpec((tm, k), lambda i, j: (i, 0)),
                  _stacked(w, layer, (k, tn), lambda i, j: (0, j)),
                  tile],
        out_specs=tuple(out_specs),
        compiler_params=_params("parallel", "arbitrary"),
        name=name,
    )(x, w, res)


def _residual_mm2_kernel(xa_ref, xb_ref, wa_ref, wb_ref, r_ref, h_ref, hb_ref, ss_ref, wst_ref,
                         *, nj):
    t = pl.program_id(0)

    def body(round_slot, use_slot):
        wst_ref[round_slot, 0] = wa_ref[...].astype(BF16)
        wst_ref[round_slot, 1] = wb_ref[...].astype(BF16)
        acc = jnp.dot(xa_ref[...], wst_ref[use_slot, 0], preferred_element_type=F32)
        acc = acc + jnp.dot(xb_ref[...], wst_ref[use_slot, 1], preferred_element_type=F32)
        _emit_stream(r_ref[...] + acc, _first_column(t, nj), h_ref, hb_ref, ss_ref)

    _staged_step(t, nj, wst_ref, body)


def residual_matmul2(xa, xb, w, layer, res, name):
    m, kx = xa.shape
    _, k, n = w.shape
    assert xb.shape == (m, kx) and k == 2 * kx
    tm = _divisor(m, _ROW_TILES)
    tn = _divisor(n, (256, 128))
    nj = n // tn
    n_steps = (m // tm) * nj + 1
    row, _, out, wcol = _staged_maps(nj, n_steps)
    tile = pl.BlockSpec((tm, tn), out)
    return pl.pallas_call(
        functools.partial(_residual_mm2_kernel, nj=nj),
        out_shape=(jax.ShapeDtypeStruct((m, n), F32), jax.ShapeDtypeStruct((m, n), BF16),
                   jax.ShapeDtypeStruct((m, LANE), F32)),
        grid=(n_steps,),
        in_specs=[pl.BlockSpec((tm, kx), row),
                  pl.BlockSpec((tm, kx), row),
                  _stacked(w, layer, (kx, tn), wcol),
                  _stacked(w, layer, (kx, tn), lambda t: (1, wcol(t)[1])),
                  tile],
        out_specs=(tile, tile, pl.BlockSpec((tm, LANE), row)),
        scratch_shapes=[pltpu.VMEM((2, 2, kx, tn), BF16)],
        compiler_params=_params("arbitrary"),
        name=name,
    )(xa, xb, w, w, res)


def _forget_logit_kernel(x_ref, ss_ref, g_ref, w_ref, b_ref, o_ref):
    acc = jnp.dot(x_ref[...], _norm_weight(w_ref, g_ref), preferred_element_type=F32)
    f = acc * _row_scale(ss_ref, x_ref.shape[1]) + b_ref[...]
    o_ref[...] = jnp.minimum(f, 0.0) - jnp.log1p(jnp.exp(-jnp.abs(f)))


def forget_log_gates(hb, ss, gain, w_f, b_f):
    m, k = hb.shape
    n = w_f.shape[1]
    assert n == LANE
    tm = _divisor(m, _ROW_TILES)
    return pl.pallas_call(
        _forget_logit_kernel,
        out_shape=jax.ShapeDtypeStruct((m, n), F32),
        grid=(m // tm,),
        in_specs=[pl.BlockSpec((tm, k), lambda i: (i, 0)),
                  pl.BlockSpec((tm, LANE), lambda i: (i, 0)),
                  pl.BlockSpec((k, 1), lambda i: (0, 0)),
                  pl.BlockSpec((k, n), lambda i: (0, 0)),
                  pl.BlockSpec((1, n), lambda i: (0, 0))],
        out_specs=pl.BlockSpec((tm, n), lambda i: (i, 0)),
        compiler_params=_params("parallel"),
        name="fox_forget_gates",
    )(hb, ss, gain.reshape(k, 1).astype(F32), w_f, b_f)


def _prefix_sum_kernel(x_ref, o_ref, *, lead):
    x = x_ref[0]
    rows = x.shape[0]
    row = lax.broadcasted_iota(jnp.int32, x.shape, 0)
    x = jnp.where(row >= lead, x, 0.0)
    shift = 1
    while shift < rows:
        x = x + jnp.where(row >= shift, pltpu.roll(x, shift, 0), 0.0)
        shift *= 2
    o_ref[0] = x


def prefix_sum_time(x, lead):
    b, rows, lanes = x.shape
    return pl.pallas_call(
        functools.partial(_prefix_sum_kernel, lead=lead),
        out_shape=jax.ShapeDtypeStruct(x.shape, F32),
        grid=(b,),
        in_specs=[pl.BlockSpec((1, rows, lanes), lambda i: (i, 0, 0))],
        out_specs=pl.BlockSpec((1, rows, lanes), lambda i: (i, 0, 0)),
        compiler_params=_params("parallel"),
        name="fox_prefix_sum",
    )(x)


def _split3(x):
    hi = x.astype(BF16).astype(F32)
    rem = x - hi
    mid = rem.astype(BF16).astype(F32)
    lo = (rem - mid).astype(BF16).astype(F32)
    return hi, mid, lo


def _fox_kernel(q_ref, k_ref, v_ref, cum_ref, o_ref, qb_ref, kb_ref, m_ref, acc_ref,
                *, lead, head_rows, tq):
    h = pl.program_id(1)
    rows, dh = q_ref.shape[1], q_ref.shape[2]
    n_main = (rows - head_rows) // tq
    lanes = cum_ref.shape[2]

    def build_bias(r0, n):
        lane = lax.broadcasted_iota(jnp.int32, (n, lanes), 1)
        c = jnp.sum(jnp.where(lane == h, cum_ref[0, pl.ds(r0, n), :], 0.0),
                    axis=-1, keepdims=True) * LOG2E
        hi, mid, lo = _split3(c)
        is_pad = (r0 + lax.broadcasted_iota(jnp.int32, (n, 1), 0)) < lead
        qb = jnp.where(lane == 0, hi, jnp.where(lane == 1, mid, jnp.where(lane == 2, lo,
             jnp.where(lane < 7, 1.0, 0.0))))
        kb = jnp.where(lane < 3, 1.0, jnp.where(lane == 3, -hi, jnp.where(lane == 4, -mid,
             jnp.where(lane == 5, -lo, jnp.where(jnp.logical_and(lane == 6, is_pad),
                                                  MASK_VALUE, 0.0)))))
        qb_ref[pl.ds(r0, n), :] = qb.astype(BF16)
        kb_ref[pl.ds(r0, n), :] = kb.astype(BF16)

    build_bias(0, head_rows)
    for i in range(n_main):
        build_bias(head_rows + i * tq, tq)

    def row_max(s):
        return jnp.broadcast_to(jnp.max(s, axis=-1, keepdims=True), (s.shape[0], lanes))

    def scores(q_aug, k0, nk):
        k_aug = jnp.concatenate([k_ref[0, pl.ds(k0, nk), :], kb_ref[pl.ds(k0, nk), :]], axis=1)
        return lax.dot_general(q_aug, k_aug, (((1,), (1,)), ((), ())),
                               preferred_element_type=F32)

    def values(k0, nk):
        return jnp.concatenate([v_ref[0, pl.ds(k0, nk), :], jnp.ones((nk, dh), BF16)], axis=1)

    def accumulate(s, s_max, v_aug):
        nq, nk = s.shape
        m_prev = m_ref[pl.ds(0, nq), :]
        m_new = jnp.maximum(m_prev, s_max)
        alpha = jnp.exp2(m_prev - m_new)
        p = jnp.exp2(s - jnp.tile(m_new, (1, nk // lanes)))
        acc_ref[pl.ds(0, nq), :] = (jnp.tile(alpha, (1, 2)) * acc_ref[pl.ds(0, nq), :]
                                    + jnp.dot(p.astype(BF16), v_aug, preferred_element_type=F32))
        m_ref[pl.ds(0, nq), :] = m_new

    def causal(s):
        return jnp.where(lax.broadcasted_iota(jnp.int32, s.shape, 1)
                         <= lax.broadcasted_iota(jnp.int32, s.shape, 0), s, MASK_VALUE)

    def reset(nq):
        m_ref[pl.ds(0, nq), :] = jnp.full((nq, lanes), MASK_VALUE, F32)
        acc_ref[pl.ds(0, nq), :] = jnp.zeros((nq, 2 * dh), F32)

    def finish(r0, nq):
        o_ref[0, pl.ds(r0, nq), :] = (acc_ref[pl.ds(0, nq), pl.ds(0, dh)]
                                      / acc_ref[pl.ds(0, nq), pl.ds(dh, dh)]).astype(o_ref.dtype)

    def q_rows(r0, nq):
        return jnp.concatenate([q_ref[0, pl.ds(r0, nq), :], qb_ref[pl.ds(r0, nq), :]], axis=1)

    reset(head_rows)
    s_head = causal(scores(q_rows(0, head_rows), 0, head_rows))
    accumulate(s_head, row_max(s_head), values(0, head_rows))
    finish(0, head_rows)

    def q_tile(i, carry):
        r0 = pl.multiple_of(head_rows + i * tq, LANE)
        q_aug = q_rows(r0, tq)
        reset(tq)

        def kv_tile(j, s):
            k0 = pl.multiple_of(head_rows + j * tq, LANE)
            s_next = scores(q_aug, pl.multiple_of(k0 + tq, LANE), tq)
            accumulate(s, row_max(s), values(k0, tq))
            return s_next

        s_diag = lax.fori_loop(0, i, kv_tile, scores(q_aug, head_rows, tq))
        s_last = jnp.concatenate([causal(s_diag), scores(q_aug, 0, head_rows)], axis=1)
        v_last = jnp.concatenate([values(r0, tq), values(0, head_rows)], axis=0)
        accumulate(s_last, row_max(s_last), v_last)
        finish(r0, tq)
        return carry

    lax.fori_loop(0, n_main, q_tile, 0)


def fox_attention(z, cum, *, batch, rows, lead, head_rows, heads, head_dim, q_col, k_col, v_col):
    z3 = z.reshape(batch, rows, z.shape[1])
    lanes = cum.shape[2]
    assert head_dim == lanes and head_rows == lanes
    tq = _divisor(rows - head_rows, (512, 256, 128))
    qb, kb, vb = q_col // head_dim, k_col // head_dim, v_col // head_dim
    kern = functools.partial(_fox_kernel, lead=lead, head_rows=head_rows, tq=tq)
    tmax = max(tq, head_rows)
    return pl.pallas_call(
        kern,
        out_shape=jax.ShapeDtypeStruct((batch, rows, heads * head_dim), BF16),
        grid=(batch, heads),
        in_specs=[pl.BlockSpec((1, rows, head_dim), lambda b, h: (b, 0, qb + h)),
                  pl.BlockSpec((1, rows, head_dim), lambda b, h: (b, 0, kb + h)),
                  pl.BlockSpec((1, rows, head_dim), lambda b, h: (b, 0, vb + h)),
                  pl.BlockSpec((1, rows, lanes), lambda b, h: (b, 0, 0))],
        out_specs=pl.BlockSpec((1, rows, head_dim), lambda b, h: (b, 0, h)),
        scratch_shapes=[pltpu.VMEM((rows, lanes), BF16), pltpu.VMEM((rows, lanes), BF16),
                        pltpu.VMEM((tmax, lanes), F32), pltpu.VMEM((tmax, 2 * head_dim), F32)],
        compiler_params=_params("parallel", "arbitrary"),
        name="fox_attention",
    )(z3, z3, z3, cum)


def _lru_kernel(x_ref, gate_ref, cw_ref, cb_ref, wa_ref, ba_ref, wx_ref, bx_ref, lam_ref,
                o_ref, ext_ref, a_ref, b_ref, carry_ref, *, lead, block_dim):
    t = pl.program_id(2)
    tl, tc = x_ref.shape[1], x_ref.shape[2]
    taps = cw_ref.shape[0]

    @pl.when(t == 0)
    def _():
        ext_ref[pl.ds(0, SUBLANE), :] = jnp.zeros((SUBLANE, tc), F32)
        carry_ref[...] = jnp.zeros_like(carry_ref)

    row = t * tl + lax.broadcasted_iota(jnp.int32, (tl, 1), 0)
    valid = row >= lead
    x = jnp.where(valid, x_ref[0].astype(F32), 0.0)
    ext_ref[pl.ds(SUBLANE, tl), :] = x

    xc = cb_ref[...] + ext_ref[pl.ds(SUBLANE - (taps - 1), tl), :] * cw_ref[pl.ds(0, 1), :]
    for j in range(1, taps):
        xc = xc + ext_ref[pl.ds(SUBLANE - (taps - 1) + j, tl), :] * cw_ref[pl.ds(j, 1), :]
    ext_ref[pl.ds(0, SUBLANE), :] = ext_ref[pl.ds(tl, SUBLANE), :]

    xcb = xc.astype(BF16)
    lam = lam_ref[...]
    softplus_neg_lam = jnp.maximum(-lam, 0.0) + jnp.log1p(jnp.exp(-jnp.abs(lam)))
    rate = -LRU_C * softplus_neg_lam
    for g in range(tc // block_dim):
        cols = slice(g * block_dim, (g + 1) * block_dim)
        xg = xcb[:, cols]
        r = jax.nn.sigmoid(jnp.dot(xg, wa_ref[g], preferred_element_type=F32) + ba_ref[:, cols])
        i = jax.nn.sigmoid(jnp.dot(xg, wx_ref[g], preferred_element_type=F32) + bx_ref[:, cols])
        log_a = rate[:, cols] * r
        a = jnp.exp(log_a)
        gated = jnp.sqrt(1.0 - jnp.exp(2.0 * log_a)) * (i * xc[:, cols])
        a_ref[:, cols] = a
        b_ref[:, cols] = jnp.where(valid, gated, 0.0)

    a = a_ref[...]
    b = b_ref[...]
    sub = lax.broadcasted_iota(jnp.int32, (tl, 1), 0) % SUBLANE
    shift = 1
    while shift < SUBLANE:
        keep = sub >= shift
        a_prev = jnp.where(keep, pltpu.roll(a, shift, 0), 1.0)
        b_prev = jnp.where(keep, pltpu.roll(b, shift, 0), 0.0)
        b = a * b_prev + b
        a = a * a_prev
        shift *= 2
    a_ref[...] = a
    b_ref[...] = b

    def group(gidx, h_prev):
        r0 = pl.multiple_of(gidx * SUBLANE, SUBLANE)
        hg = a_ref[pl.ds(r0, SUBLANE), :] * h_prev + b_ref[pl.ds(r0, SUBLANE), :]
        b_ref[pl.ds(r0, SUBLANE), :] = hg
        return hg[SUBLANE - 1:SUBLANE, :]

    carry_ref[...] = lax.fori_loop(0, tl // SUBLANE, group, carry_ref[...], unroll=4)

    gate = gate_ref[0].astype(F32)
    o_ref[0] = (b_ref[...] * jax.nn.gelu(gate)).astype(o_ref.dtype)


def lru_branch(z, conv_w, conv_b, w_a, b_a, w_x, b_x, lam, *, batch, rows, lead, width):
    z3 = z.reshape(batch, rows, z.shape[1])
    nblk, bd, _ = w_a.shape
    tc = _divisor(width, (512, 256, 128))
    tl = _divisor(rows, (384, 256, 128, 64, 8))
    nct = width // tc
    row2 = lambda v: v.reshape(1, width).astype(F32)
    kern = functools.partial(_lru_kernel, lead=lead, block_dim=bd)
    vec = pl.BlockSpec((1, tc), lambda b, c, t: (0, c))
    return pl.pallas_call(
        kern,
        out_shape=jax.ShapeDtypeStruct((batch, rows, width), BF16),
        grid=(batch, nct, rows // tl),
        in_specs=[pl.BlockSpec((1, tl, tc), lambda b, c, t: (b, t, c)),
                  pl.BlockSpec((1, tl, tc), lambda b, c, t: (b, t, nct + c)),
                  pl.BlockSpec((conv_w.shape[0], tc), lambda b, c, t: (0, c)),
                  vec,
                  pl.BlockSpec((tc // bd, bd, bd), lambda b, c, t: (c, 0, 0)),
                  vec,
                  pl.BlockSpec((tc // bd, bd, bd), lambda b, c, t: (c, 0, 0)),
                  vec, vec],
        out_specs=pl.BlockSpec((1, tl, tc), lambda b, c, t: (b, t, c)),
        scratch_shapes=[pltpu.VMEM((tl + SUBLANE, tc), F32), pltpu.VMEM((tl, tc), F32),
                        pltpu.VMEM((tl, tc), F32), pltpu.VMEM((1, tc), F32)],
        compiler_params=_params("parallel", "parallel", "arbitrary"),
        name="rg_lru",
    )(z3, z3, conv_w.astype(F32), row2(conv_b), w_a.astype(BF16), row2(b_a),
      w_x.astype(BF16), row2(b_x), row2(lam))


def _retention_kernel(lg_ref, q_ref, k_ref, v_ref, g_ref, cos_ref, sin_ref, gain_ref, o_ref,
                      s_ref, d_ref, *, lead, tb, chunk):
    h = pl.program_id(1)
    rows = q_ref.shape[1]
    dk = q_ref.shape[2]
    half = dk // 2
    lg = lg_ref[h]

    ti = lax.broadcasted_iota(jnp.int32, (tb, tb), 0)
    si = lax.broadcasted_iota(jnp.int32, (tb, tb), 1)
    dist = jnp.abs(ti - si).astype(F32)
    d_ref[...] = jnp.where(si // chunk <= ti // chunk, jnp.exp(lg * dist), 0.0)
    idx = lax.broadcasted_iota(jnp.int32, (tb, 1), 0).astype(F32)
    q_decay = jnp.exp(lg * (idx + 1.0))
    k_decay = jnp.exp(lg * (tb - 1.0 - idx))
    block_decay = jnp.exp(lg * jnp.full((1, 1), float(tb), F32))
    s_ref[...] = jnp.zeros_like(s_ref)

    def rotate(x, c, s):
        x1, x2 = x[:, :half], x[:, half:]
        return jnp.concatenate([x1 * c - x2 * s, x1 * s + x2 * c], axis=1)

    def block(n, carry):
        r0 = pl.multiple_of(n * tb, LANE)
        valid = (r0 + lax.broadcasted_iota(jnp.int32, (tb, 1), 0)) >= lead
        c = cos_ref[pl.ds(r0, tb), :]
        s = sin_ref[pl.ds(r0, tb), :]
        q = rotate(q_ref[0, pl.ds(r0, tb), :].astype(F32), c, s)
        k = rotate(k_ref[0, pl.ds(r0, tb), :].astype(F32), c, s) * (dk ** -0.5)
        k = jnp.where(valid, k, 0.0)
        v = v_ref[0, pl.ds(r0, tb), :]
        v = jnp.where(valid, v, jnp.zeros_like(v))

        w = lax.dot_general(q.astype(BF16), k.astype(BF16), (((1,), (1,)), ((), ())),
                            preferred_element_type=F32) * d_ref[...]
        state = s_ref[...]
        o = jnp.dot(w.astype(BF16), v, preferred_element_type=F32)
        o = o + jnp.dot((q * q_decay).astype(BF16), state.astype(BF16),
                        preferred_element_type=F32)
        s_ref[...] = state * block_decay + lax.dot_general(
            (k * k_decay).astype(BF16), v, (((0,), (0,)), ((), ())),
            preferred_element_type=F32)

        ms = jnp.mean(o * o, axis=-1, keepdims=True)
        on = o * lax.rsqrt(ms + RMS_EPS) * gain_ref[0]
        g = g_ref[0, pl.ds(r0, tb), :].astype(F32)
        o_ref[0, pl.ds(r0, tb), :] = (g * jax.nn.sigmoid(g) * on).astype(o_ref.dtype)
        return carry

    lax.fori_loop(0, rows // tb, block, 0)


def retention(z, cos, sin, ret_norm, *, batch, rows, lead, heads):
    cols = z.shape[1]
    z3 = z.reshape(batch, rows, cols)
    vw = ret_norm.shape[0]
    qk = (cols - 2 * vw) // 2
    dk, dv = qk // heads, vw // heads
    tb = _divisor(rows, (384, 128, 64))
    assert tb % RET_CHUNK == 0
    log_gamma = jnp.log(1.0 - 2.0 ** (-5.0 - jnp.arange(heads, dtype=F32)))
    kern = functools.partial(_retention_kernel, lead=lead, tb=tb, chunk=RET_CHUNK)
    kb, vb, gb = qk // dk, 2 * qk // dv, (2 * qk + vw) // dv
    return pl.pallas_call(
        kern,
        out_shape=jax.ShapeDtypeStruct((batch, rows, vw), BF16),
        grid=(batch, heads),
        in_specs=[pl.BlockSpec(memory_space=pltpu.SMEM),
                  pl.BlockSpec((1, rows, dk), lambda b, h: (b, 0, h)),
                  pl.BlockSpec((1, rows, dk), lambda b, h: (b, 0, kb + h)),
                  pl.BlockSpec((1, rows, dv), lambda b, h: (b, 0, vb + h)),
                  pl.BlockSpec((1, rows, dv), lambda b, h: (b, 0, gb + h)),
                  pl.BlockSpec((rows, dk // 2), lambda b, h: (0, 0)),
                  pl.BlockSpec((rows, dk // 2), lambda b, h: (0, 0)),
                  pl.BlockSpec((1, 1, dv), lambda b, h: (h, 0, 0))],
        out_specs=pl.BlockSpec((1, rows, dv), lambda b, h: (b, 0, h)),
        scratch_shapes=[pltpu.VMEM((dk, dv), F32), pltpu.VMEM((tb, tb), F32)],
        compiler_params=_params("parallel", "arbitrary"),
        name="retention",
    )(log_gamma, z3, z3, z3, z3, cos, sin, ret_norm.reshape(heads, 1, dv).astype(F32))


def _ffn(stream, layer, norm, w_gate, w_up, w_down_bf16, last):
    h, hb, ss = stream
    a = norm_swiglu_up(hb, ss, norm, w_gate, w_up, layer)
    return residual_matmul(a, w_down_bf16, layer, h, "ffn_down", emit_norm=not last)


def _lru_fox_layer(stream, geom, j, norm, w_in_all, b_f, conv_w, conv_b, w_a, b_a, w_x, b_x, lam,
                   q_norm, k_norm, w_out_all):
    h, hb, ss = stream
    batch, rows, lead, head_rows = geom
    w_in = w_in_all[j]
    heads = b_f.shape[0]
    head_dim = q_norm.shape[0]
    fox_w = heads * head_dim
    lru_w = lam.shape[0]
    q_col, k_col, v_col, f_col = 2 * lru_w, 2 * lru_w + fox_w, 2 * lru_w + 2 * fox_w, 2 * lru_w + 3 * fox_w

    gains = jnp.ones((1, f_col), F32)
    gains = gains.at[0, q_col:k_col].set(
        jnp.tile(q_norm.astype(F32) * (head_dim ** -0.5 * LOG2E), heads))
    gains = gains.at[0, k_col:v_col].set(jnp.tile(k_norm.astype(F32), heads))
    z = norm_matmul(hb, ss, norm, w_in_all, j, f_col, "lru_fox_in_proj",
                    head_norm=(gains, (q_col, v_col), head_dim))

    w_f = jnp.zeros((w_in.shape[0], LANE), F32).at[:, :heads].set(w_in[:, f_col:])
    bias_f = jnp.zeros((1, LANE), F32).at[0, :heads].set(b_f.astype(F32))
    log_f = forget_log_gates(hb, ss, norm, w_f, bias_f).reshape(batch, rows, LANE)
    cum = prefix_sum_time(log_f, lead)

    y_lru = lru_branch(z, conv_w, conv_b, w_a, b_a, w_x, b_x, lam,
                       batch=batch, rows=rows, lead=lead, width=lru_w)
    y_fox = fox_attention(z, cum, batch=batch, rows=rows, lead=lead, head_rows=head_rows,
                          heads=heads, head_dim=head_dim, q_col=q_col, k_col=k_col, v_col=v_col)
    m = batch * rows
    return residual_matmul2(y_lru.reshape(m, lru_w), y_fox.reshape(m, fox_w), w_out_all, j, h,
                            "lru_fox_out_proj")


def _retention_layer(stream, geom, j, norm, w_in_all, ret_norm, w_out_bf16):
    h, hb, ss = stream
    batch, rows, lead, _ = geom
    vw = ret_norm.shape[0]
    n_in = w_in_all.shape[2]
    dk = (n_in - 2 * vw) // 2 // RET_HEADS
    z = norm_matmul(hb, ss, norm, w_in_all, j, n_in, "retention_in_proj")
    half = dk // 2
    inv = ROPE_BASE ** (-jnp.arange(half, dtype=F32) / half)
    pos = (jnp.arange(rows) - lead).astype(F32)
    ang = pos[:, None] * inv[None, :]
    y = retention(z, jnp.cos(ang), jnp.sin(ang), ret_norm, batch=batch, rows=rows, lead=lead,
                  heads=RET_HEADS)
    return residual_matmul(y.reshape(batch * rows, vw), w_out_bf16, j, h, "retention_out_proj")


def kernel(x, meta_tokens, ab_norm, ab_w_in, ab_b_f, ab_conv_w, ab_conv_b, ab_w_a, ab_b_a, ab_w_x, ab_b_x, ab_lambda, ab_q_norm, ab_k_norm, ab_w_out, c_norm, c_w_in, c_ret_norm, c_w_out, ffn_norm, ffn_w_gate, ffn_w_up, ffn_w_down):
    batch, seq, d = x.shape
    n_meta = meta_tokens.shape[0]
    depth = ffn_norm.shape[0]
    lead = (-n_meta) % LANE
    head_rows = lead + n_meta
    rows = head_rows + seq
    assert seq % LANE == 0 and lead % RET_CHUNK == (-n_meta) % RET_CHUNK
    geom = (batch, rows, lead, head_rows)
    m = batch * rows

    head = jnp.concatenate([jnp.zeros((lead, d), F32), meta_tokens.astype(F32)], axis=0)
    stream = tuple(a.reshape(m, a.shape[-1]) for a in embed(x.astype(F32), head))
    w_down_bf16 = ffn_w_down.astype(BF16)
    c_w_out_bf16 = c_w_out.astype(BF16)
    for layer in range(depth):
        j = layer // 2
        if layer % 2 == 0:
            stream = _lru_fox_layer(stream, geom, j, ab_norm[j], ab_w_in, ab_b_f[j], ab_conv_w[j],
                                    ab_conv_b[j], ab_w_a[j], ab_b_a[j], ab_w_x[j], ab_b_x[j],
                                    ab_lambda[j], ab_q_norm[j], ab_k_norm[j], ab_w_out)
        else:
            stream = _retention_layer(stream, geom, j, c_norm[j], c_w_in, c_ret_norm[j], c_w_out_bf16)
        stream = _ffn(stream, layer, ffn_norm[layer], ffn_w_gate, ffn_w_up, w_down_bf16,
                      last=layer == depth - 1)
    return stream[0].reshape(batch, rows, d)[:, head_rows:]
```

```python
import functools

import jax
import jax.numpy as jnp
from jax import lax
from jax.experimental import pallas as pl
from jax.experimental.pallas import tpu as pltpu

F32 = jnp.float32
BF16 = jnp.bfloat16

LANE = 128
SUBLANE = 8
VMEM_LIMIT_BYTES = 60 * 1024 * 1024

RMS_EPS = 1e-6
LRU_C = 8.0
RET_CHUNK = 64
RET_HEADS = 16
ROPE_BASE = 10000.0
MASK_VALUE = -1e30
LOG2E = 1.4426950408889634


def _divisor(n, candidates):
    for c in candidates:
        if n % c == 0:
            return c
    raise ValueError(f"no tile in {candidates} divides {n}")


def _params(*semantics):
    return pltpu.CompilerParams(dimension_semantics=semantics,
                                vmem_limit_bytes=VMEM_LIMIT_BYTES)


def _lane_partial_sums(sq):
    part = sq[:, :LANE]
    for c in range(1, sq.shape[1] // LANE):
        part = part + sq[:, c * LANE:(c + 1) * LANE]
    return part


def _embed_kernel(head_ref, x_ref, h_ref, hb_ref, ss_ref):
    t = pl.program_id(1)

    def emit(v):
        h_ref[0] = v
        hb_ref[0] = v.astype(BF16)
        ss_ref[0] = _lane_partial_sums(v * v)

    @pl.when(t == 0)
    def _():
        emit(head_ref[...])

    @pl.when(t > 0)
    def _():
        emit(x_ref[0])


def embed(x, head):
    batch, seq, d = x.shape
    hr = head.shape[0]
    rows = hr + seq
    nt = rows // hr
    blk = lambda b, t: (b, t, 0)
    return pl.pallas_call(
        _embed_kernel,
        out_shape=(jax.ShapeDtypeStruct((batch, rows, d), F32),
                   jax.ShapeDtypeStruct((batch, rows, d), BF16),
                   jax.ShapeDtypeStruct((batch, rows, LANE), F32)),
        grid=(batch, nt),
        in_specs=[pl.BlockSpec((hr, d), lambda b, t: (0, 0)),
                  pl.BlockSpec((1, hr, d), lambda b, t: (b, jnp.maximum(t - 1, 0), 0))],
        out_specs=(pl.BlockSpec((1, hr, d), blk), pl.BlockSpec((1, hr, d), blk),
                   pl.BlockSpec((1, hr, LANE), blk)),
        compiler_params=_params("parallel", "arbitrary"),
        name="embed",
    )(head, x)


def _row_scale(ss_ref, d):
    ms = jnp.sum(ss_ref[...], axis=-1, keepdims=True) * (1.0 / d)
    return jnp.broadcast_to(lax.rsqrt(ms + RMS_EPS), ss_ref.shape)


def _scaled(acc, rstd_ref):
    return acc * jnp.tile(rstd_ref[...], (1, acc.shape[1] // LANE))


def _norm_weight(w_ref, g_ref):
    return (w_ref[...] * g_ref[...]).astype(BF16)


_ROW_TILES = (1536, 768, 512, 256, 128, 8)


def _staged_maps(nj, n_steps):
    def product(t):
        u = jnp.maximum(t - 1, 0)
        return u // nj, u % nj
    return (lambda t: (product(t)[0], 0), lambda t: (0, product(t)[1]), product,
            lambda t: (0, jnp.minimum(t, n_steps - 2) % nj))


def _staged_step(t, nj, wb_ref, body):
    @pl.when(t == 0)
    def _():
        wb_ref[...] = jnp.zeros_like(wb_ref)

    @pl.when(t % 2 == 0)
    def _():
        body(0, 1)

    @pl.when(t % 2 == 1)
    def _():
        body(1, 0)


def _first_column(t, nj):
    return jnp.maximum(t - 1, 0) % nj == 0


def _norm_mm_kernel(x_ref, ss_ref, g_ref, w_ref, o_ref, rstd_ref, wb_ref, *, nj):
    t = pl.program_id(0)

    @pl.when(_first_column(t, nj))
    def _():
        rstd_ref[...] = _row_scale(ss_ref, x_ref.shape[1])

    def body(round_slot, use_slot):
        wb_ref[round_slot] = _norm_weight(w_ref, g_ref)
        acc = jnp.dot(x_ref[...], wb_ref[use_slot], preferred_element_type=F32)
        o_ref[...] = _scaled(acc, rstd_ref).astype(o_ref.dtype)

    _staged_step(t, nj, wb_ref, body)


def _norm_headnorm_mm_kernel(x_ref, ss_ref, g_ref, w_ref, hg_ref, o_ref, rstd_ref, wb_ref,
                             *, nj, norm_lo, norm_hi, head_dim):
    t = pl.program_id(0)
    j = jnp.maximum(t - 1, 0) % nj

    @pl.when(_first_column(t, nj))
    def _():
        rstd_ref[...] = _row_scale(ss_ref, x_ref.shape[1])

    def body(round_slot, use_slot):
        wb_ref[round_slot] = _norm_weight(w_ref, g_ref)
        acc = jnp.dot(x_ref[...], wb_ref[use_slot], preferred_element_type=F32)
        acc = _scaled(acc, rstd_ref)
        is_norm = jnp.logical_and(j >= norm_lo, j < norm_hi)

        @pl.when(is_norm)
        def _():
            for h in range(acc.shape[1] // head_dim):
                cols = slice(h * head_dim, (h + 1) * head_dim)
                y = acc[:, cols]
                ms = jnp.mean(y * y, axis=-1, keepdims=True)
                o_ref[:, cols] = (y * lax.rsqrt(ms + RMS_EPS) * hg_ref[:, cols]).astype(o_ref.dtype)

        @pl.when(jnp.logical_not(is_norm))
        def _():
            o_ref[...] = acc.astype(o_ref.dtype)

    _staged_step(t, nj, wb_ref, body)


def _stacked(w, layer, block, index_map):
    return pl.BlockSpec((None,) + block, lambda *idx: (layer,) + tuple(index_map(*idx)))


def norm_matmul(hb, ss, gain, w, layer, n_out, name, head_norm=None):
    m, k = hb.shape
    tm = _divisor(m, _ROW_TILES)
    tn = _divisor(n_out, (512, 256, 128))
    nj = n_out // tn
    n_steps = (m // tm) * nj + 1
    row, col, out, wcol = _staged_maps(nj, n_steps)
    in_specs = [pl.BlockSpec((tm, k), row, pipeline_mode=pl.Buffered(1)),
                pl.BlockSpec((tm, LANE), row),
                pl.BlockSpec((k, 1), lambda t: (0, 0)),
                _stacked(w, layer, (k, tn), wcol)]
    args = [hb, ss, gain.reshape(k, 1).astype(F32), w]
    if head_norm is None:
        kern = functools.partial(_norm_mm_kernel, nj=nj)
    else:
        gains, (lo, hi), head_dim = head_norm
        assert tn % head_dim == 0 and lo % tn == 0 and hi % tn == 0
        kern = functools.partial(_norm_headnorm_mm_kernel, nj=nj, norm_lo=lo // tn,
                                 norm_hi=hi // tn, head_dim=head_dim)
        in_specs.append(pl.BlockSpec((1, tn), col))
        args.append(gains)
    return pl.pallas_call(
        kern,
        out_shape=jax.ShapeDtypeStruct((m, n_out), BF16),
        grid=(n_steps,),
        in_specs=in_specs,
        out_specs=pl.BlockSpec((tm, tn), out),
        scratch_shapes=[pltpu.VMEM((tm, LANE), F32), pltpu.VMEM((2, k, tn), BF16)],
        compiler_params=_params("arbitrary"),
        name=name,
    )(*args)


def _norm_swiglu_kernel(x_ref, ss_ref, g_ref, wg_ref, wu_ref, o_ref, rstd_ref, wb_ref, *, nj):
    t = pl.program_id(0)

    @pl.when(_first_column(t, nj))
    def _():
        rstd_ref[...] = _row_scale(ss_ref, x_ref.shape[1])

    def body(round_slot, use_slot):
        wb_ref[round_slot, 0] = _norm_weight(wg_ref, g_ref)
        wb_ref[round_slot, 1] = _norm_weight(wu_ref, g_ref)
        x = x_ref[...]
        gate = _scaled(jnp.dot(x, wb_ref[use_slot, 0], preferred_element_type=F32), rstd_ref)
        up = _scaled(jnp.dot(x, wb_ref[use_slot, 1], preferred_element_type=F32), rstd_ref)
        o_ref[...] = (gate * jax.nn.sigmoid(gate) * up).astype(o_ref.dtype)

    _staged_step(t, nj, wb_ref, body)


def norm_swiglu_up(hb, ss, gain, w_gate, w_up, layer):
    m, k = hb.shape
    n = w_gate.shape[2]
    tm = _divisor(m, _ROW_TILES)
    tn = _divisor(n, (256, 128))
    nj = n // tn
    n_steps = (m // tm) * nj + 1
    row, _, out, wcol = _staged_maps(nj, n_steps)
    return pl.pallas_call(
        functools.partial(_norm_swiglu_kernel, nj=nj),
        out_shape=jax.ShapeDtypeStruct((m, n), BF16),
        grid=(n_steps,),
        in_specs=[pl.BlockSpec((tm, k), row),
                  pl.BlockSpec((tm, LANE), row),
                  pl.BlockSpec((k, 1), lambda t: (0, 0)),
                  _stacked(w_gate, layer, (k, tn), wcol),
                  _stacked(w_up, layer, (k, tn), wcol)],
        out_specs=pl.BlockSpec((tm, tn), out),
        scratch_shapes=[pltpu.VMEM((tm, LANE), F32), pltpu.VMEM((2, 2, k, tn), BF16)],
        compiler_params=_params("arbitrary"),
        name="swiglu_up",
    )(hb, ss, gain.reshape(k, 1).astype(F32), w_gate, w_up)


def _emit_stream(h_new, first_column, h_ref, hb_ref, ss_ref):
    h_ref[...] = h_new
    hb_ref[...] = h_new.astype(BF16)
    part = _lane_partial_sums(h_new * h_new)

    @pl.when(first_column)
    def _():
        ss_ref[...] = part

    @pl.when(jnp.logical_not(first_column))
    def _():
        ss_ref[...] += part


def _residual_mm_kernel(x_ref, w_ref, r_ref, *out_refs):
    h_new = r_ref[...] + jnp.dot(x_ref[...], w_ref[...], preferred_element_type=F32)
    if len(out_refs) == 1:
        out_refs[0][...] = h_new
    else:
        _emit_stream(h_new, pl.program_id(1) == 0, *out_refs)


def residual_matmul(x, w, layer, res, name, emit_norm=True):
    m, k = x.shape
    n = w.shape[2]
    tm = _divisor(m, _ROW_TILES[1:])
    tn = _divisor(n, (512, 256, 128) if k <= 8192 else (256, 128))
    tile = pl.BlockSpec((tm, tn), lambda i, j: (i, j))
    out_shape = [jax.ShapeDtypeStruct((m, n), F32)]
    out_specs = [tile]
    if emit_norm:
        out_shape += [jax.ShapeDtypeStruct((m, n), BF16), jax.ShapeDtypeStruct((m, LANE), F32)]
        out_specs += [tile, pl.BlockSpec((tm, LANE), lambda i, j: (i, 0))]
    return pl.pallas_call(
        _residual_mm_kernel,
        out_shape=tuple(out_shape),
        grid=(m // tm, n // tn),
        in_specs=[pl.BlockSpec((tm, k), lambda i, j: (i, 0)),
                  _stacked(w, layer, (k, tn), lambda i, j: (0, j)),
                  tile],
        out_specs=tuple(out_specs),
        compiler_params=_params("parallel", "arbitrary"),
        name=name,
    )(x, w, res)


def _residual_mm2_kernel(xa_ref, xb_ref, wa_ref, wb_ref, r_ref, h_ref, hb_ref, ss_ref, wst_ref,
                         *, nj):
    t = pl.program_id(0)

    def body(round_slot, use_slot):
        wst_ref[round_slot, 0] = wa_ref[...].astype(BF16)
        wst_ref[round_slot, 1] = wb_ref[...].astype(BF16)
        acc = jnp.dot(xa_ref[...], wst_ref[use_slot, 0], preferred_element_type=F32)
        acc = acc + jnp.dot(xb_ref[...], wst_ref[use_slot, 1], preferred_element_type=F32)
        _emit_stream(r_ref[...] + acc, _first_column(t, nj), h_ref, hb_ref, ss_ref)

    _staged_step(t, nj, wst_ref, body)


def residual_matmul2(xa, xb, w, layer, res, name):
    m, kx = xa.shape
    _, k, n = w.shape
    assert xb.shape == (m, kx) and k == 2 * kx
    tm = _divisor(m, _ROW_TILES)
    tn = _divisor(n, (512, 256, 128))
    nj = n // tn
    n_steps = (m // tm) * nj + 1
    row, _, out, wcol = _staged_maps(nj, n_steps)
    tile = pl.BlockSpec((tm, tn), out)
    return pl.pallas_call(
        functools.partial(_residual_mm2_kernel, nj=nj),
        out_shape=(jax.ShapeDtypeStruct((m, n), F32), jax.ShapeDtypeStruct((m, n), BF16),
                   jax.ShapeDtypeStruct((m, LANE), F32)),
        grid=(n_steps,),
        in_specs=[pl.BlockSpec((tm, kx), row, pipeline_mode=pl.Buffered(1)),
                  pl.BlockSpec((tm, kx), row, pipeline_mode=pl.Buffered(1)),
                  _stacked(w, layer, (kx, tn), wcol),
                  _stacked(w, layer, (kx, tn), lambda t: (1, wcol(t)[1])),
                  tile],
        out_specs=(tile, tile, pl.BlockSpec((tm, LANE), row)),
        scratch_shapes=[pltpu.VMEM((2, 2, kx, tn), BF16)],
        compiler_params=_params("arbitrary"),
        name=name,
    )(xa, xb, w, w, res)


def _forget_logit_kernel(x_ref, ss_ref, g_ref, w_ref, b_ref, o_ref):
    acc = jnp.dot(x_ref[...], _norm_weight(w_ref, g_ref), preferred_element_type=F32)
    f = acc * _row_scale(ss_ref, x_ref.shape[1]) + b_ref[...]
    o_ref[...] = jnp.minimum(f, 0.0) - jnp.log1p(jnp.exp(-jnp.abs(f)))


def forget_log_gates(hb, ss, gain, w_f, b_f):
    m, k = hb.shape
    n = w_f.shape[1]
    assert n == LANE
    tm = _divisor(m, _ROW_TILES)
    return pl.pallas_call(
        _forget_logit_kernel,
        out_shape=jax.ShapeDtypeStruct((m, n), F32),
        grid=(m // tm,),
        in_specs=[pl.BlockSpec((tm, k), lambda i: (i, 0)),
                  pl.BlockSpec((tm, LANE), lambda i: (i, 0)),
                  pl.BlockSpec((k, 1), lambda i: (0, 0)),
                  pl.BlockSpec((k, n), lambda i: (0, 0)),
                  pl.BlockSpec((1, n), lambda i: (0, 0))],
        out_specs=pl.BlockSpec((tm, n), lambda i: (i, 0)),
        compiler_params=_params("parallel"),
        name="fox_forget_gates",
    )(hb, ss, gain.reshape(k, 1).astype(F32), w_f, b_f)


def _prefix_sum_kernel(x_ref, o_ref, *, lead):
    x = x_ref[0]
    rows = x.shape[0]
    row = lax.broadcasted_iota(jnp.int32, x.shape, 0)
    x = jnp.where(row >= lead, x, 0.0)
    shift = 1
    while shift < rows:
        x = x + jnp.where(row >= shift, pltpu.roll(x, shift, 0), 0.0)
        shift *= 2
    o_ref[0] = x


def prefix_sum_time(x, lead):
    b, rows, lanes = x.shape
    return pl.pallas_call(
        functools.partial(_prefix_sum_kernel, lead=lead),
        out_shape=jax.ShapeDtypeStruct(x.shape, F32),
        grid=(b,),
        in_specs=[pl.BlockSpec((1, rows, lanes), lambda i: (i, 0, 0))],
        out_specs=pl.BlockSpec((1, rows, lanes), lambda i: (i, 0, 0)),
        compiler_params=_params("parallel"),
        name="fox_prefix_sum",
    )(x)


def _split3(x):
    hi = x.astype(BF16).astype(F32)
    rem = x - hi
    mid = rem.astype(BF16).astype(F32)
    lo = (rem - mid).astype(BF16).astype(F32)
    return hi, mid, lo


def _fox_kernel(q_ref, k_ref, v_ref, cum_ref, o_ref, qb_ref, kb_ref, m_ref, acc_ref,
                *, lead, head_rows, tq):
    h = pl.program_id(1)
    rows, dh = q_ref.shape[1], q_ref.shape[2]
    n_main = (rows - head_rows) // tq
    lanes = cum_ref.shape[2]

    def build_bias(r0, n):
        lane = lax.broadcasted_iota(jnp.int32, (n, lanes), 1)
        c = jnp.sum(jnp.where(lane == h, cum_ref[0, pl.ds(r0, n), :], 0.0),
                    axis=-1, keepdims=True) * LOG2E
        hi, mid, lo = _split3(c)
        is_pad = (r0 + lax.broadcasted_iota(jnp.int32, (n, 1), 0)) < lead
        qb = jnp.where(lane == 0, hi, jnp.where(lane == 1, mid, jnp.where(lane == 2, lo,
             jnp.where(lane < 7, 1.0, 0.0))))
        kb = jnp.where(lane < 3, 1.0, jnp.where(lane == 3, -hi, jnp.where(lane == 4, -mid,
             jnp.where(lane == 5, -lo, jnp.where(jnp.logical_and(lane == 6, is_pad),
                                                  MASK_VALUE, 0.0)))))
        qb_ref[pl.ds(r0, n), :] = qb.astype(BF16)
        kb_ref[pl.ds(r0, n), :] = kb.astype(BF16)

    build_bias(0, head_rows)
    for i in range(n_main):
        build_bias(head_rows + i * tq, tq)

    def row_max(s):
        return jnp.broadcast_to(jnp.max(s, axis=-1, keepdims=True), (s.shape[0], lanes))

    def scores(q_aug, k0, nk):
        k_aug = jnp.concatenate([k_ref[0, pl.ds(k0, nk), :], kb_ref[pl.ds(k0, nk), :]], axis=1)
        return lax.dot_general(q_aug, k_aug, (((1,), (1,)), ((), ())),
                               preferred_element_type=F32)

    def values(k0, nk):
        return jnp.concatenate([v_ref[0, pl.ds(k0, nk), :], jnp.ones((nk, dh), BF16)], axis=1)

    def accumulate(s, s_max, v_aug):
        nq, nk = s.shape
        m_prev = m_ref[pl.ds(0, nq), :]
        m_new = jnp.maximum(m_prev, s_max)
        alpha = jnp.exp2(m_prev - m_new)
        p = jnp.exp2(s - jnp.tile(m_new, (1, nk // lanes)))
        acc_ref[pl.ds(0, nq), :] = (jnp.tile(alpha, (1, 2)) * acc_ref[pl.ds(0, nq), :]
                                    + jnp.dot(p.astype(BF16), v_aug, preferred_element_type=F32))
        m_ref[pl.ds(0, nq), :] = m_new

    def causal(s):
        return jnp.where(lax.broadcasted_iota(jnp.int32, s.shape, 1)
                         <= lax.broadcasted_iota(jnp.int32, s.shape, 0), s, MASK_VALUE)

    def reset(nq):
        m_ref[pl.ds(0, nq), :] = jnp.full((nq, lanes), MASK_VALUE, F32)
        acc_ref[pl.ds(0, nq), :] = jnp.zeros((nq, 2 * dh), F32)

    def finish(r0, nq):
        o_ref[0, pl.ds(r0, nq), :] = (acc_ref[pl.ds(0, nq), pl.ds(0, dh)]
                                      / acc_ref[pl.ds(0, nq), pl.ds(dh, dh)]).astype(o_ref.dtype)

    def q_rows(r0, nq):
        return jnp.concatenate([q_ref[0, pl.ds(r0, nq), :], qb_ref[pl.ds(r0, nq), :]], axis=1)

    reset(head_rows)
    s_head = causal(scores(q_rows(0, head_rows), 0, head_rows))
    accumulate(s_head, row_max(s_head), values(0, head_rows))
    finish(0, head_rows)

    def q_tile(i, carry):
        r0 = pl.multiple_of(head_rows + i * tq, LANE)
        q_aug = q_rows(r0, tq)
        reset(tq)

        def kv_tile(j, s):
            k0 = pl.multiple_of(head_rows + j * tq, LANE)
            s_next = scores(q_aug, pl.multiple_of(k0 + tq, LANE), tq)
            accumulate(s, row_max(s), values(k0, tq))
            return s_next

        s_diag = lax.fori_loop(0, i, kv_tile, scores(q_aug, head_rows, tq))
        s_last = jnp.concatenate([causal(s_diag), scores(q_aug, 0, head_rows)], axis=1)
        v_last = jnp.concatenate([values(r0, tq), values(0, head_rows)], axis=0)
        accumulate(s_last, row_max(s_last), v_last)
        finish(r0, tq)
        return carry

    lax.fori_loop(0, n_main, q_tile, 0)


def fox_attention(z, cum, *, batch, rows, lead, head_rows, heads, head_dim, q_col, k_col, v_col):
    z3 = z.reshape(batch, rows, z.shape[1])
    lanes = cum.shape[2]
    assert head_dim == lanes and head_rows == lanes
    tq = _divisor(rows - head_rows, (512, 256, 128))
    qb, kb, vb = q_col // head_dim, k_col // head_dim, v_col // head_dim
    kern = functools.partial(_fox_kernel, lead=lead, head_rows=head_rows, tq=tq)
    tmax = max(tq, head_rows)
    return pl.pallas_call(
        kern,
        out_shape=jax.ShapeDtypeStruct((batch, rows, heads * head_dim), BF16),
        grid=(batch, heads),
        in_specs=[pl.BlockSpec((1, rows, head_dim), lambda b, h: (b, 0, qb + h)),
                  pl.BlockSpec((1, rows, head_dim), lambda b, h: (b, 0, kb + h)),
                  pl.BlockSpec((1, rows, head_dim), lambda b, h: (b, 0, vb + h)),
                  pl.BlockSpec((1, rows, lanes), lambda b, h: (b, 0, 0))],
        out_specs=pl.BlockSpec((1, rows, head_dim), lambda b, h: (b, 0, h)),
        scratch_shapes=[pltpu.VMEM((rows, lanes), BF16), pltpu.VMEM((rows, lanes), BF16),
                        pltpu.VMEM((tmax, lanes), F32), pltpu.VMEM((tmax, 2 * head_dim), F32)],
        compiler_params=_params("parallel", "arbitrary"),
        name="fox_attention",
    )(z3, z3, z3, cum)


def _lru_kernel(x_ref, gate_ref, cw_ref, cb_ref, wa_ref, ba_ref, wx_ref, bx_ref, lam_ref,
                o_ref, ext_ref, h_ref, carry_ref, *, lead):
    t = pl.program_id(2)
    tl, tc = x_ref.shape[1], x_ref.shape[2]
    taps = cw_ref.shape[0]
    seg = tl // SUBLANE
    n_slab = tc // LANE
    top = min(tl, -(-lead // SUBLANE) * SUBLANE)

    @pl.when(t == 0)
    def _():
        ext_ref[:, pl.ds(0, SUBLANE), :] = jnp.zeros((n_slab, SUBLANE, LANE), F32)
        carry_ref[...] = jnp.zeros_like(carry_ref)

    x = x_ref[0].astype(F32)
    row_top = t * tl + lax.broadcasted_iota(jnp.int32, (top, 1), 0)
    x = jnp.concatenate([jnp.where(row_top >= lead, x[:top], 0.0), x[top:]], axis=0)
    lam = lam_ref[...]
    rate = -LRU_C * (jnp.maximum(-lam, 0.0) + jnp.log1p(jnp.exp(-jnp.abs(lam))))
    sub = lax.broadcasted_iota(jnp.int32, (SUBLANE, 1), 0)

    for c in range(n_slab):
        cols = slice(c * LANE, (c + 1) * LANE)
        ext_ref[c, pl.ds(SUBLANE, tl), :] = x[:, cols]

        def rows_at(g):
            return ext_ref[c, pl.ds(SUBLANE + g, SUBLANE, stride=seg), :]

        shifted = {g: rows_at(g) for g in range(-(taps - 1), seg)}
        conv = []
        for g in range(seg):
            acc = cb_ref[:, cols] + shifted[g - (taps - 1)] * cw_ref[pl.ds(0, 1), cols]
            for j in range(1, taps):
                acc = acc + shifted[g - (taps - 1) + j] * cw_ref[pl.ds(j, 1), cols]
            conv.append(acc)
        xc = jnp.concatenate(conv, axis=0)
        ext_ref[c, pl.ds(0, SUBLANE), :] = ext_ref[c, pl.ds(tl, SUBLANE), :]

        xcb = xc.astype(BF16)
        r = jax.nn.sigmoid(jnp.dot(xcb, wa_ref[c], preferred_element_type=F32) + ba_ref[:, cols])
        i = jax.nn.sigmoid(jnp.dot(xcb, wx_ref[c], preferred_element_type=F32) + bx_ref[:, cols])
        log_a = rate[:, cols] * r
        a = jnp.exp(log_a)
        b = jnp.sqrt(1.0 - jnp.exp(2.0 * log_a)) * (i * xc)

        h_loc, prod = [], []
        for g in range(seg):
            rows = slice(g * SUBLANE, (g + 1) * SUBLANE)
            b_g = jnp.where(t * tl + sub * seg + g >= lead, b[rows], 0.0)
            h_loc.append(b_g if g == 0 else a[rows] * h_loc[-1] + b_g)
            prod.append(a[rows] if g == 0 else a[rows] * prod[-1])

        state = [carry_ref[:, cols]]
        for s_ in range(SUBLANE):
            state.append(prod[-1][s_:s_ + 1] * state[-1] + h_loc[-1][s_:s_ + 1])
        carry_ref[:, cols] = state[SUBLANE]
        entering = jnp.concatenate(state[:SUBLANE], axis=0)
        for g in range(seg):
            h_ref[c, pl.ds(g, SUBLANE, stride=seg), :] = h_loc[g] + prod[g] * entering

    gate = gate_ref[0].astype(F32)
    for c in range(n_slab):
        cols = slice(c * LANE, (c + 1) * LANE)
        o_ref[0, :, cols] = (h_ref[c] * jax.nn.gelu(gate[:, cols])).astype(o_ref.dtype)


def lru_branch(z, conv_w, conv_b, w_a, b_a, w_x, b_x, lam, *, batch, rows, lead, width):
    z3 = z.reshape(batch, rows, z.shape[1])
    nblk, bd, _ = w_a.shape
    assert bd == LANE
    tc = _divisor(width, (512, 256, 128))
    tl = _divisor(rows, (352, 96, 32, 8))
    nct = width // tc
    row2 = lambda v: v.reshape(1, width).astype(F32)
    kern = functools.partial(_lru_kernel, lead=lead)
    vec = pl.BlockSpec((1, tc), lambda b, c, t: (0, c))
    return pl.pallas_call(
        kern,
        out_shape=jax.ShapeDtypeStruct((batch, rows, width), BF16),
        grid=(batch, nct, rows // tl),
        in_specs=[pl.BlockSpec((1, tl, tc), lambda b, c, t: (b, t, c)),
                  pl.BlockSpec((1, tl, tc), lambda b, c, t: (b, t, nct + c)),
                  pl.BlockSpec((conv_w.shape[0], tc), lambda b, c, t: (0, c)),
                  vec,
                  pl.BlockSpec((tc // bd, bd, bd), lambda b, c, t: (c, 0, 0)),
                  vec,
                  pl.BlockSpec((tc // bd, bd, bd), lambda b, c, t: (c, 0, 0)),
                  vec, vec],
        out_specs=pl.BlockSpec((1, tl, tc), lambda b, c, t: (b, t, c)),
        scratch_shapes=[pltpu.VMEM((tc // LANE, tl + SUBLANE, LANE), F32),
                        pltpu.VMEM((tc // LANE, tl, LANE), F32), pltpu.VMEM((1, tc), F32)],
        compiler_params=_params("parallel", "parallel", "arbitrary"),
        name="rg_lru",
    )(z3, z3, conv_w.astype(F32), row2(conv_b), w_a.astype(BF16), row2(b_a),
      w_x.astype(BF16), row2(b_x), row2(lam))


def _retention_kernel(lg_ref, q_ref, k_ref, v_ref, g_ref, cos_ref, sin_ref, gain_ref, o_ref,
                      s_ref, d_ref, *, lead, tb, chunk):
    h = pl.program_id(1)
    rows = q_ref.shape[1]
    dk = q_ref.shape[2]
    half = dk // 2
    lg = lg_ref[h]

    ti = lax.broadcasted_iota(jnp.int32, (tb, tb), 0)
    si = lax.broadcasted_iota(jnp.int32, (tb, tb), 1)
    dist = jnp.abs(ti - si).astype(F32)
    d_ref[...] = jnp.where(si // chunk <= ti // chunk, jnp.exp(lg * dist), 0.0)
    idx = lax.broadcasted_iota(jnp.int32, (tb, 1), 0).astype(F32)
    q_decay = jnp.exp(lg * (idx + 1.0))
    k_decay = jnp.exp(lg * (tb - 1.0 - idx))
    block_decay = jnp.exp(lg * jnp.full((1, 1), float(tb), F32))
    s_ref[...] = jnp.zeros_like(s_ref)

    def rotate(x, c, s):
        x1, x2 = x[:, :half], x[:, half:]
        return jnp.concatenate([x1 * c - x2 * s, x1 * s + x2 * c], axis=1)

    def block(n, carry):
        r0 = pl.multiple_of(n * tb, LANE)
        valid = (r0 + lax.broadcasted_iota(jnp.int32, (tb, 1), 0)) >= lead
        c = cos_ref[pl.ds(r0, tb), :]
        s = sin_ref[pl.ds(r0, tb), :]
        q = rotate(q_ref[0, pl.ds(r0, tb), :].astype(F32), c, s)
        k = rotate(k_ref[0, pl.ds(r0, tb), :].astype(F32), c, s) * (dk ** -0.5)
        k = jnp.where(valid, k, 0.0)
        v = v_ref[0, pl.ds(r0, tb), :]
        v = jnp.where(valid, v, jnp.zeros_like(v))

        w = lax.dot_general(q.astype(BF16), k.astype(BF16), (((1,), (1,)), ((), ())),
                            preferred_element_type=F32) * d_ref[...]
        state = s_ref[...]
        o = jnp.dot(w.astype(BF16), v, preferred_element_type=F32)
        o = o + jnp.dot((q * q_decay).astype(BF16), state.astype(BF16),
                        preferred_element_type=F32)
        s_ref[...] = state * block_decay + lax.dot_general(
            (k * k_decay).astype(BF16), v, (((0,), (0,)), ((), ())),
            preferred_element_type=F32)

        ms = jnp.mean(o * o, axis=-1, keepdims=True)
        on = o * lax.rsqrt(ms + RMS_EPS) * gain_ref[0]
        g = g_ref[0, pl.ds(r0, tb), :].astype(F32)
        o_ref[0, pl.ds(r0, tb), :] = (g * jax.nn.sigmoid(g) * on).astype(o_ref.dtype)
        return carry

    lax.fori_loop(0, rows // tb, block, 0)


def retention(z, cos, sin, ret_norm, *, batch, rows, lead, heads):
    cols = z.shape[1]
    z3 = z.reshape(batch, rows, cols)
    vw = ret_norm.shape[0]
    qk = (cols - 2 * vw) // 2
    dk, dv = qk // heads, vw // heads
    tb = _divisor(rows, (384, 128, 64))
    assert tb % RET_CHUNK == 0
    log_gamma = jnp.log(1.0 - 2.0 ** (-5.0 - jnp.arange(heads, dtype=F32)))
    kern = functools.partial(_retention_kernel, lead=lead, tb=tb, chunk=RET_CHUNK)
    kb, vb, gb = qk // dk, 2 * qk // dv, (2 * qk + vw) // dv
    return pl.pallas_call(
        kern,
        out_shape=jax.ShapeDtypeStruct((batch, rows, vw), BF16),
        grid=(batch, heads),
        in_specs=[pl.BlockSpec(memory_space=pltpu.SMEM),
                  pl.BlockSpec((1, rows, dk), lambda b, h: (b, 0, h)),
                  pl.BlockSpec((1, rows, dk), lambda b, h: (b, 0, kb + h)),
                  pl.BlockSpec((1, rows, dv), lambda b, h: (b, 0, vb + h)),
                  pl.BlockSpec((1, rows, dv), lambda b, h: (b, 0, gb + h)),
                  pl.BlockSpec((rows, dk // 2), lambda b, h: (0, 0)),
                  pl.BlockSpec((rows, dk // 2), lambda b, h: (0, 0)),
                  pl.BlockSpec((1, 1, dv), lambda b, h: (h, 0, 0))],
        out_specs=pl.BlockSpec((1, rows, dv), lambda b, h: (b, 0, h)),
        scratch_shapes=[pltpu.VMEM((dk, dv), F32), pltpu.VMEM((tb, tb), F32)],
        compiler_params=_params("parallel", "arbitrary"),
        name="retention",
    )(log_gamma, z3, z3, z3, z3, cos, sin, ret_norm.reshape(heads, 1, dv).astype(F32))


def _ffn(stream, layer, norm, w_gate, w_up, w_down_bf16, last):
    h, hb, ss = stream
    a = norm_swiglu_up(hb, ss, norm, w_gate, w_up, layer)
    return residual_matmul(a, w_down_bf16, layer, h, "ffn_down", emit_norm=not last)


def _lru_fox_layer(stream, geom, j, norm, w_in_all, b_f, conv_w, conv_b, w_a, b_a, w_x, b_x, lam,
                   q_norm, k_norm, w_out_all):
    h, hb, ss = stream
    batch, rows, lead, head_rows = geom
    w_in = w_in_all[j]
    heads = b_f.shape[0]
    head_dim = q_norm.shape[0]
    fox_w = heads * head_dim
    lru_w = lam.shape[0]
    q_col, k_col, v_col, f_col = 2 * lru_w, 2 * lru_w + fox_w, 2 * lru_w + 2 * fox_w, 2 * lru_w + 3 * fox_w

    gains = jnp.ones((1, f_col), F32)
    gains = gains.at[0, q_col:k_col].set(
        jnp.tile(q_norm.astype(F32) * (head_dim ** -0.5 * LOG2E), heads))
    gains = gains.at[0, k_col:v_col].set(jnp.tile(k_norm.astype(F32), heads))
    z = norm_matmul(hb, ss, norm, w_in_all, j, f_col, "lru_fox_in_proj",
                    head_norm=(gains, (q_col, v_col), head_dim))

    w_f = jnp.zeros((w_in.shape[0], LANE), F32).at[:, :heads].set(w_in[:, f_col:])
    bias_f = jnp.zeros((1, LANE), F32).at[0, :heads].set(b_f.astype(F32))
    log_f = forget_log_gates(hb, ss, norm, w_f, bias_f).reshape(batch, rows, LANE)
    cum = prefix_sum_time(log_f, lead)

    y_lru = lru_branch(z, conv_w, conv_b, w_a, b_a, w_x, b_x, lam,
                       batch=batch, rows=rows, lead=lead, width=lru_w)
    y_fox = fox_attention(z, cum, batch=batch, rows=rows, lead=lead, head_rows=head_rows,
                          heads=heads, head_dim=head_dim, q_col=q_col, k_col=k_col, v_col=v_col)
    m = batch * rows
    return residual_matmul2(y_lru.reshape(m, lru_w), y_fox.reshape(m, fox_w), w_out_all, j, h,
                            "lru_fox_out_proj")


def _retention_layer(stream, geom, j, norm, w_in_all, ret_norm, w_out_bf16):
    h, hb, ss = stream
    batch, rows, lead, _ = geom
    vw = ret_norm.shape[0]
    n_in = w_in_all.shape[2]
    dk = (n_in - 2 * vw) // 2 // RET_HEADS
    z = norm_matmul(hb, ss, norm, w_in_all, j, n_in, "retention_in_proj")
    half = dk // 2
    inv = ROPE_BASE ** (-jnp.arange(half, dtype=F32) / half)
    pos = (jnp.arange(rows) - lead).astype(F32)
    ang = pos[:, None] * inv[None, :]
    y = retention(z, jnp.cos(ang), jnp.sin(ang), ret_norm, batch=batch, rows=rows, lead=lead,
                  heads=RET_HEADS)
    return residual_matmul(y.reshape(batch * rows, vw), w_out_bf16, j, h, "retention_out_proj")


def kernel(x, meta_tokens, ab_norm, ab_w_in, ab_b_f, ab_conv_w, ab_conv_b, ab_w_a, ab_b_a, ab_w_x, ab_b_x, ab_lambda, ab_q_norm, ab_k_norm, ab_w_out, c_norm, c_w_in, c_ret_norm, c_w_out, ffn_norm, ffn_w_gate, ffn_w_up, ffn_w_down):
    batch, seq, d = x.shape
    n_meta = meta_tokens.shape[0]
    depth = ffn_norm.shape[0]
    lead = (-n_meta) % LANE
    head_rows = lead + n_meta
    rows = head_rows + seq
    assert seq % LANE == 0 and lead % RET_CHUNK == (-n_meta) % RET_CHUNK
    geom = (batch, rows, lead, head_rows)
    m = batch * rows

    head = jnp.concatenate([jnp.zeros((lead, d), F32), meta_tokens.astype(F32)], axis=0)
    stream = tuple(a.reshape(m, a.shape[-1]) for a in embed(x.astype(F32), head))
    w_down_bf16 = ffn_w_down.astype(BF16)
    c_w_out_bf16 = c_w_out.astype(BF16)
    for layer in range(depth):
        j = layer // 2
        if layer % 2 == 0:
            stream = _lru_fox_layer(stream, geom, j, ab_norm[j], ab_w_in, ab_b_f[j], ab_conv_w[j],
                                    ab_conv_b[j], ab_w_a[j], ab_b_a[j], ab_w_x[j], ab_b_x[j],
                                    ab_lambda[j], ab_q_norm[j], ab_k_norm[j], ab_w_out)
        else:
            stream = _retention_layer(stream, geom, j, c_norm[j], c_w_in, c_ret_norm[j], c_w_out_bf16)
        stream = _ffn(stream, layer, ffn_norm[layer], ffn_w_gate, ffn_w_up, w_down_bf16,
                      last=layer == depth - 1)
    return stream[0].reshape(batch, rows, d)[:, head_rows:]
```

```python
import functools

import jax
import jax.numpy as jnp
from jax import lax
from jax.experimental import pallas as pl
from jax.experimental.pallas import tpu as pltpu

F32 = jnp.float32
BF16 = jnp.bfloat16

LANE = 128
SUBLANE = 8
VMEM_LIMIT_BYTES = 60 * 1024 * 1024

RMS_EPS = 1e-6
LRU_C = 8.0
RET_CHUNK = 64
RET_HEADS = 16
ROPE_BASE = 10000.0
MASK_VALUE = -1e30
LOG2E = 1.4426950408889634


def _divisor(n, candidates):
    for c in candidates:
        if n % c == 0:
            return c
    raise ValueError(f"no tile in {candidates} divides {n}")


def _params(*semantics):
    return pltpu.CompilerParams(dimension_semantics=semantics,
                                vmem_limit_bytes=VMEM_LIMIT_BYTES)


def _lane_partial_sums(sq):
    part = sq[:, :LANE]
    for c in range(1, sq.shape[1] // LANE):
        part = part + sq[:, c * LANE:(c + 1) * LANE]
    return part


def _embed_kernel(head_ref, x_ref, h_ref, hb_ref, ss_ref):
    t = pl.program_id(1)

    def emit(v):
        h_ref[0] = v
        hb_ref[0] = v.astype(BF16)
        ss_ref[0] = _lane_partial_sums(v * v)

    @pl.when(t == 0)
    def _():
        emit(head_ref[...])

    @pl.when(t > 0)
    def _():
        emit(x_ref[0])


def embed(x, head):
    batch, seq, d = x.shape
    hr = head.shape[0]
    rows = hr + seq
    nt = rows // hr
    blk = lambda b, t: (b, t, 0)
    return pl.pallas_call(
        _embed_kernel,
        out_shape=(jax.ShapeDtypeStruct((batch, rows, d), F32),
                   jax.ShapeDtypeStruct((batch, rows, d), BF16),
                   jax.ShapeDtypeStruct((batch, rows, LANE), F32)),
        grid=(batch, nt),
        in_specs=[pl.BlockSpec((hr, d), lambda b, t: (0, 0)),
                  pl.BlockSpec((1, hr, d), lambda b, t: (b, jnp.maximum(t - 1, 0), 0))],
        out_specs=(pl.BlockSpec((1, hr, d), blk), pl.BlockSpec((1, hr, d), blk),
                   pl.BlockSpec((1, hr, LANE), blk)),
        compiler_params=_params("parallel", "arbitrary"),
        name="embed",
    )(head, x)


def _row_scale(ss_ref, d):
    ms = jnp.sum(ss_ref[...], axis=-1, keepdims=True) * (1.0 / d)
    return jnp.broadcast_to(lax.rsqrt(ms + RMS_EPS), ss_ref.shape)


def _scaled(acc, rstd_ref):
    return acc * jnp.tile(rstd_ref[...], (1, acc.shape[1] // LANE))


def _norm_weight(w_ref, g_ref):
    return (w_ref[...] * g_ref[...]).astype(BF16)


_ROW_TILES = (1536, 768, 512, 256, 128, 8)


def _staged_maps(nj, n_steps):
    def product(t):
        u = jnp.maximum(t - 1, 0)
        return u // nj, u % nj
    return (lambda t: (product(t)[0], 0), lambda t: (0, product(t)[1]), product,
            lambda t: (0, jnp.minimum(t, n_steps - 2) % nj))


def _staged_step(t, nj, wb_ref, body):
    @pl.when(t == 0)
    def _():
        wb_ref[...] = jnp.zeros_like(wb_ref)

    @pl.when(t % 2 == 0)
    def _():
        body(0, 1)

    @pl.when(t % 2 == 1)
    def _():
        body(1, 0)


def _first_column(t, nj):
    return jnp.maximum(t - 1, 0) % nj == 0


def _norm_mm_kernel(x_ref, ss_ref, g_ref, w_ref, o_ref, rstd_ref, wb_ref, *, nj):
    t = pl.program_id(0)

    @pl.when(_first_column(t, nj))
    def _():
        rstd_ref[...] = _row_scale(ss_ref, x_ref.shape[1])

    def body(round_slot, use_slot):
        wb_ref[round_slot] = _norm_weight(w_ref, g_ref)
        acc = jnp.dot(x_ref[...], wb_ref[use_slot], preferred_element_type=F32)
        o_ref[...] = _scaled(acc, rstd_ref).astype(o_ref.dtype)

    _staged_step(t, nj, wb_ref, body)


def _headnorm_mm_kernel(x_ref, ss_ref, w_ref, hg_ref, o_ref, rstd_ref, *, norm_lo, norm_hi, head_dim):
    j = pl.program_id(1)

    @pl.when(j == 0)
    def _():
        rstd_ref[...] = _row_scale(ss_ref, x_ref.shape[1])

    acc = _scaled(jnp.dot(x_ref[...], w_ref[...], preferred_element_type=F32), rstd_ref)
    is_norm = jnp.logical_and(j >= norm_lo, j < norm_hi)

    @pl.when(is_norm)
    def _():
        for h in range(acc.shape[1] // head_dim):
            cols = slice(h * head_dim, (h + 1) * head_dim)
            y = acc[:, cols]
            ms = jnp.mean(y * y, axis=-1, keepdims=True)
            o_ref[:, cols] = (y * lax.rsqrt(ms + RMS_EPS) * hg_ref[:, cols]).astype(o_ref.dtype)

    @pl.when(jnp.logical_not(is_norm))
    def _():
        o_ref[...] = acc.astype(o_ref.dtype)


def headnorm_matmul(hb, ss, w, gains, norm_cols, head_dim, name):
    m, k = hb.shape
    n = w.shape[1]
    tm = _divisor(m, _ROW_TILES)
    tn = _divisor(n, (512, 256, 128))
    lo, hi = norm_cols
    assert tn % head_dim == 0 and lo % tn == 0 and hi % tn == 0
    kern = functools.partial(_headnorm_mm_kernel, norm_lo=lo // tn, norm_hi=hi // tn,
                             head_dim=head_dim)
    return pl.pallas_call(
        kern,
        out_shape=jax.ShapeDtypeStruct((m, n), BF16),
        grid=(m // tm, n // tn),
        in_specs=[pl.BlockSpec((tm, k), lambda i, j: (i, 0)),
                  pl.BlockSpec((tm, LANE), lambda i, j: (i, 0)),
                  pl.BlockSpec((k, tn), lambda i, j: (0, j)),
                  pl.BlockSpec((1, tn), lambda i, j: (0, j))],
        out_specs=pl.BlockSpec((tm, tn), lambda i, j: (i, j)),
        scratch_shapes=[pltpu.VMEM((tm, LANE), F32)],
        compiler_params=_params("parallel", "arbitrary"),
        name=name,
    )(hb, ss, w, gains)


def _stacked(w, layer, block, index_map):
    return pl.BlockSpec((None,) + block, lambda *idx: (layer,) + tuple(index_map(*idx)))


def norm_matmul(hb, ss, gain, w, layer, name):
    m, k = hb.shape
    n_out = w.shape[2]
    tm = _divisor(m, _ROW_TILES)
    tn = _divisor(n_out, (512, 256, 128))
    nj = n_out // tn
    n_steps = (m // tm) * nj + 1
    row, _, out, wcol = _staged_maps(nj, n_steps)
    return pl.pallas_call(
        functools.partial(_norm_mm_kernel, nj=nj),
        out_shape=jax.ShapeDtypeStruct((m, n_out), BF16),
        grid=(n_steps,),
        in_specs=[pl.BlockSpec((tm, k), row, pipeline_mode=pl.Buffered(1)),
                  pl.BlockSpec((tm, LANE), row),
                  pl.BlockSpec((k, 1), lambda t: (0, 0)),
                  _stacked(w, layer, (k, tn), wcol)],
        out_specs=pl.BlockSpec((tm, tn), out),
        scratch_shapes=[pltpu.VMEM((tm, LANE), F32), pltpu.VMEM((2, k, tn), BF16)],
        compiler_params=_params("arbitrary"),
        name=name,
    )(hb, ss, gain.reshape(k, 1).astype(F32), w)


def _norm_swiglu_kernel(x_ref, ss_ref, g_ref, wg_ref, wu_ref, o_ref, rstd_ref, wb_ref, *, nj):
    t = pl.program_id(0)

    @pl.when(_first_column(t, nj))
    def _():
        rstd_ref[...] = _row_scale(ss_ref, x_ref.shape[1])

    def body(round_slot, use_slot):
        wb_ref[round_slot, 0] = _norm_weight(wg_ref, g_ref)
        wb_ref[round_slot, 1] = _norm_weight(wu_ref, g_ref)
        x = x_ref[...]
        gate = _scaled(jnp.dot(x, wb_ref[use_slot, 0], preferred_element_type=F32), rstd_ref)
        up = _scaled(jnp.dot(x, wb_ref[use_slot, 1], preferred_element_type=F32), rstd_ref)
        o_ref[...] = (gate * jax.nn.sigmoid(gate) * up).astype(o_ref.dtype)

    _staged_step(t, nj, wb_ref, body)


def norm_swiglu_up(hb, ss, gain, w_gate, w_up, layer):
    m, k = hb.shape
    n = w_gate.shape[2]
    tm = _divisor(m, _ROW_TILES)
    tn = _divisor(n, (256, 128))
    nj = n // tn
    n_steps = (m // tm) * nj + 1
    row, _, out, wcol = _staged_maps(nj, n_steps)
    return pl.pallas_call(
        functools.partial(_norm_swiglu_kernel, nj=nj),
        out_shape=jax.ShapeDtypeStruct((m, n), BF16),
        grid=(n_steps,),
        in_specs=[pl.BlockSpec((tm, k), row),
                  pl.BlockSpec((tm, LANE), row),
                  pl.BlockSpec((k, 1), lambda t: (0, 0)),
                  _stacked(w_gate, layer, (k, tn), wcol),
                  _stacked(w_up, layer, (k, tn), wcol)],
        out_specs=pl.BlockSpec((tm, tn), out),
        scratch_shapes=[pltpu.VMEM((tm, LANE), F32), pltpu.VMEM((2, 2, k, tn), BF16)],
        compiler_params=_params("arbitrary"),
        name="swiglu_up",
    )(hb, ss, gain.reshape(k, 1).astype(F32), w_gate, w_up)


def _emit_stream(h_new, first_column, h_ref, hb_ref, ss_ref):
    h_ref[...] = h_new
    hb_ref[...] = h_new.astype(BF16)
    part = _lane_partial_sums(h_new * h_new)

    @pl.when(first_column)
    def _():
        ss_ref[...] = part

    @pl.when(jnp.logical_not(first_column))
    def _():
        ss_ref[...] += part


def _residual_mm_kernel(x_ref, w_ref, r_ref, *out_refs):
    h_new = r_ref[...] + jnp.dot(x_ref[...], w_ref[...], preferred_element_type=F32)
    if len(out_refs) == 1:
        out_refs[0][...] = h_new
    else:
        _emit_stream(h_new, pl.program_id(1) == 0, *out_refs)


def residual_matmul(x, w, layer, res, name, emit_norm=True):
    m, k = x.shape
    n = w.shape[2]
    tm = _divisor(m, _ROW_TILES[1:])
    tn = _divisor(n, (512, 256, 128) if k <= 8192 else (256, 128))
    tile = pl.BlockSpec((tm, tn), lambda i, j: (i, j))
    out_shape = [jax.ShapeDtypeStruct((m, n), F32)]
    out_specs = [tile]
    if emit_norm:
        out_shape += [jax.ShapeDtypeStruct((m, n), BF16), jax.ShapeDtypeStruct((m, LANE), F32)]
        out_specs += [tile, pl.BlockSpec((tm, LANE), lambda i, j: (i, 0))]
    return pl.pallas_call(
        _residual_mm_kernel,
        out_shape=tuple(out_shape),
        grid=(m // tm, n // tn),
        in_specs=[pl.BlockSpec((tm, k), lambda i, j: (i, 0)),
                  _stacked(w, layer, (k, tn), lambda i, j: (0, j)),
                  tile],
        out_specs=tuple(out_specs),
        compiler_params=_params("parallel", "arbitrary"),
        name=name,
    )(x, w, res)


def _residual_mm2_kernel(xa_ref, xb_ref, wa_ref, wb_ref, r_ref, h_ref, hb_ref, ss_ref):
    acc = jnp.dot(xa_ref[...], wa_ref[...], preferred_element_type=F32)
    acc = acc + jnp.dot(xb_ref[...], wb_ref[...], preferred_element_type=F32)
    _emit_stream(r_ref[...] + acc, pl.program_id(1) == 0, h_ref, hb_ref, ss_ref)


def residual_matmul2(xa, xb, w, res, name):
    m, kx = xa.shape
    k, n = w.shape
    assert xb.shape == (m, kx) and k == 2 * kx
    tm = _divisor(m, _ROW_TILES)
    tn = _divisor(n, (512, 256, 128))
    tile = pl.BlockSpec((tm, tn), lambda i, j: (i, j))
    return pl.pallas_call(
        _residual_mm2_kernel,
        out_shape=(jax.ShapeDtypeStruct((m, n), F32), jax.ShapeDtypeStruct((m, n), BF16),
                   jax.ShapeDtypeStruct((m, LANE), F32)),
        grid=(m // tm, n // tn),
        in_specs=[pl.BlockSpec((tm, kx), lambda i, j: (i, 0)),
                  pl.BlockSpec((tm, kx), lambda i, j: (i, 0)),
                  pl.BlockSpec((kx, tn), lambda i, j: (0, j)),
                  pl.BlockSpec((kx, tn), lambda i, j: (1, j)),
                  tile],
        out_specs=(tile, tile, pl.BlockSpec((tm, LANE), lambda i, j: (i, 0))),
        compiler_params=_params("parallel", "arbitrary"),
        name=name,
    )(xa, xb, w, w, res)


def _forget_logit_kernel(x_ref, ss_ref, g_ref, w_ref, b_ref, o_ref):
    acc = jnp.dot(x_ref[...], _norm_weight(w_ref, g_ref), preferred_element_type=F32)
    f = acc * _row_scale(ss_ref, x_ref.shape[1]) + b_ref[...]
    o_ref[...] = jnp.minimum(f, 0.0) - jnp.log1p(jnp.exp(-jnp.abs(f)))


def forget_log_gates(hb, ss, gain, w_f, b_f):
    m, k = hb.shape
    n = w_f.shape[1]
    assert n == LANE
    tm = _divisor(m, _ROW_TILES)
    return pl.pallas_call(
        _forget_logit_kernel,
        out_shape=jax.ShapeDtypeStruct((m, n), F32),
        grid=(m // tm,),
        in_specs=[pl.BlockSpec((tm, k), lambda i: (i, 0)),
                  pl.BlockSpec((tm, LANE), lambda i: (i, 0)),
                  pl.BlockSpec((k, 1), lambda i: (0, 0)),
                  pl.BlockSpec((k, n), lambda i: (0, 0)),
                  pl.BlockSpec((1, n), lambda i: (0, 0))],
        out_specs=pl.BlockSpec((tm, n), lambda i: (i, 0)),
        compiler_params=_params("parallel"),
        name="fox_forget_gates",
    )(hb, ss, gain.reshape(k, 1).astype(F32), w_f, b_f)


def _prefix_sum_kernel(x_ref, o_ref, *, lead):
    x = x_ref[0]
    rows = x.shape[0]
    row = lax.broadcasted_iota(jnp.int32, x.shape, 0)
    x = jnp.where(row >= lead, x, 0.0)
    shift = 1
    while shift < rows:
        x = x + jnp.where(row >= shift, pltpu.roll(x, shift, 0), 0.0)
        shift *= 2
    o_ref[0] = x


def prefix_sum_time(x, lead):
    b, rows, lanes = x.shape
    return pl.pallas_call(
        functools.partial(_prefix_sum_kernel, lead=lead),
        out_shape=jax.ShapeDtypeStruct(x.shape, F32),
        grid=(b,),
        in_specs=[pl.BlockSpec((1, rows, lanes), lambda i: (i, 0, 0))],
        out_specs=pl.BlockSpec((1, rows, lanes), lambda i: (i, 0, 0)),
        compiler_params=_params("parallel"),
        name="fox_prefix_sum",
    )(x)


def _split3(x):
    hi = x.astype(BF16).astype(F32)
    rem = x - hi
    mid = rem.astype(BF16).astype(F32)
    lo = (rem - mid).astype(BF16).astype(F32)
    return hi, mid, lo


def _fox_kernel(q_ref, k_ref, v_ref, cum_ref, o_ref, qb_ref, kb_ref, m_ref, acc_ref,
                *, lead, head_rows, tq):
    h = pl.program_id(1)
    rows, dh = q_ref.shape[1], q_ref.shape[2]
    n_main = (rows - head_rows) // tq
    lanes = cum_ref.shape[2]

    def build_bias(r0, n):
        lane = lax.broadcasted_iota(jnp.int32, (n, lanes), 1)
        c = jnp.sum(jnp.where(lane == h, cum_ref[0, pl.ds(r0, n), :], 0.0),
                    axis=-1, keepdims=True) * LOG2E
        hi, mid, lo = _split3(c)
        is_pad = (r0 + lax.broadcasted_iota(jnp.int32, (n, 1), 0)) < lead
        qb = jnp.where(lane == 0, hi, jnp.where(lane == 1, mid, jnp.where(lane == 2, lo,
             jnp.where(lane < 7, 1.0, 0.0))))
        kb = jnp.where(lane < 3, 1.0, jnp.where(lane == 3, -hi, jnp.where(lane == 4, -mid,
             jnp.where(lane == 5, -lo, jnp.where(jnp.logical_and(lane == 6, is_pad),
                                                  MASK_VALUE, 0.0)))))
        qb_ref[pl.ds(r0, n), :] = qb.astype(BF16)
        kb_ref[pl.ds(r0, n), :] = kb.astype(BF16)

    build_bias(0, head_rows)
    for i in range(n_main):
        build_bias(head_rows + i * tq, tq)

    def row_max(s):
        return jnp.broadcast_to(jnp.max(s, axis=-1, keepdims=True), (s.shape[0], lanes))

    def scores(q_aug, k0, nk):
        k_aug = jnp.concatenate([k_ref[0, pl.ds(k0, nk), :], kb_ref[pl.ds(k0, nk), :]], axis=1)
        return lax.dot_general(q_aug, k_aug, (((1,), (1,)), ((), ())),
                               preferred_element_type=F32)

    def values(k0, nk):
        return jnp.concatenate([v_ref[0, pl.ds(k0, nk), :], jnp.ones((nk, dh), BF16)], axis=1)

    def accumulate(s, s_max, v_aug):
        nq, nk = s.shape
        m_prev = m_ref[pl.ds(0, nq), :]
        m_new = jnp.maximum(m_prev, s_max)
        alpha = jnp.exp2(m_prev - m_new)
        p = jnp.exp2(s - jnp.tile(m_new, (1, nk // lanes)))
        acc_ref[pl.ds(0, nq), :] = (jnp.tile(alpha, (1, 2)) * acc_ref[pl.ds(0, nq), :]
                                    + jnp.dot(p.astype(BF16), v_aug, preferred_element_type=F32))
        m_ref[pl.ds(0, nq), :] = m_new

    def causal(s):
        return jnp.where(lax.broadcasted_iota(jnp.int32, s.shape, 1)
                         <= lax.broadcasted_iota(jnp.int32, s.shape, 0), s, MASK_VALUE)

    def reset(nq):
        m_ref[pl.ds(0, nq), :] = jnp.full((nq, lanes), MASK_VALUE, F32)
        acc_ref[pl.ds(0, nq), :] = jnp.zeros((nq, 2 * dh), F32)

    def finish(r0, nq):
        o_ref[0, pl.ds(r0, nq), :] = (acc_ref[pl.ds(0, nq), pl.ds(0, dh)]
                                      / acc_ref[pl.ds(0, nq), pl.ds(dh, dh)]).astype(o_ref.dtype)

    def q_rows(r0, nq):
        return jnp.concatenate([q_ref[0, pl.ds(r0, nq), :], qb_ref[pl.ds(r0, nq), :]], axis=1)

    reset(head_rows)
    s_head = causal(scores(q_rows(0, head_rows), 0, head_rows))
    accumulate(s_head, row_max(s_head), values(0, head_rows))
    finish(0, head_rows)

    def q_tile(i, carry):
        r0 = pl.multiple_of(head_rows + i * tq, LANE)
        q_aug = q_rows(r0, tq)
        reset(tq)

        def kv_tile(j, s):
            k0 = pl.multiple_of(head_rows + j * tq, LANE)
            s_next = scores(q_aug, pl.multiple_of(k0 + tq, LANE), tq)
            accumulate(s, row_max(s), values(k0, tq))
            return s_next

        s_diag = lax.fori_loop(0, i, kv_tile, scores(q_aug, head_rows, tq))
        s_last = jnp.concatenate([causal(s_diag), scores(q_aug, 0, head_rows)], axis=1)
        v_last = jnp.concatenate([values(r0, tq), values(0, head_rows)], axis=0)
        accumulate(s_last, row_max(s_last), v_last)
        finish(r0, tq)
        return carry

    lax.fori_loop(0, n_main, q_tile, 0)


def fox_attention(z, cum, *, batch, rows, lead, head_rows, heads, head_dim, q_col, k_col, v_col):
    z3 = z.reshape(batch, rows, z.shape[1])
    lanes = cum.shape[2]
    assert head_dim == lanes and head_rows == lanes
    tq = _divisor(rows - head_rows, (512, 256, 128))
    qb, kb, vb = q_col // head_dim, k_col // head_dim, v_col // head_dim
    kern = functools.partial(_fox_kernel, lead=lead, head_rows=head_rows, tq=tq)
    tmax = max(tq, head_rows)
    return pl.pallas_call(
        kern,
        out_shape=jax.ShapeDtypeStruct((batch, rows, heads * head_dim), BF16),
        grid=(batch, heads),
        in_specs=[pl.BlockSpec((1, rows, head_dim), lambda b, h: (b, 0, qb + h)),
                  pl.BlockSpec((1, rows, head_dim), lambda b, h: (b, 0, kb + h)),
                  pl.BlockSpec((1, rows, head_dim), lambda b, h: (b, 0, vb + h)),
                  pl.BlockSpec((1, rows, lanes), lambda b, h: (b, 0, 0))],
        out_specs=pl.BlockSpec((1, rows, head_dim), lambda b, h: (b, 0, h)),
        scratch_shapes=[pltpu.VMEM((rows, lanes), BF16), pltpu.VMEM((rows, lanes), BF16),
                        pltpu.VMEM((tmax, lanes), F32), pltpu.VMEM((tmax, 2 * head_dim), F32)],
        compiler_params=_params("parallel", "arbitrary"),
        name="fox_attention",
    )(z3, z3, z3, cum)


def _lru_kernel(x_ref, gate_ref, cw_ref, cb_ref, wa_ref, ba_ref, wx_ref, bx_ref, lam_ref,
                o_ref, ext_ref, h_ref, carry_ref, *, lead):
    t = pl.program_id(2)
    tl, tc = x_ref.shape[1], x_ref.shape[2]
    taps = cw_ref.shape[0]
    seg = tl // SUBLANE
    n_slab = tc // LANE
    top = min(tl, -(-lead // SUBLANE) * SUBLANE)

    @pl.when(t == 0)
    def _():
        ext_ref[:, pl.ds(0, SUBLANE), :] = jnp.zeros((n_slab, SUBLANE, LANE), F32)
        carry_ref[...] = jnp.zeros_like(carry_ref)

    x = x_ref[0].astype(F32)
    row_top = t * tl + lax.broadcasted_iota(jnp.int32, (top, 1), 0)
    x = jnp.concatenate([jnp.where(row_top >= lead, x[:top], 0.0), x[top:]], axis=0)
    lam = lam_ref[...]
    rate = -LRU_C * (jnp.maximum(-lam, 0.0) + jnp.log1p(jnp.exp(-jnp.abs(lam))))
    sub = lax.broadcasted_iota(jnp.int32, (SUBLANE, 1), 0)

    for c in range(n_slab):
        cols = slice(c * LANE, (c + 1) * LANE)
        ext_ref[c, pl.ds(SUBLANE, tl), :] = x[:, cols]

        def rows_at(g):
            return ext_ref[c, pl.ds(SUBLANE + g, SUBLANE, stride=seg), :]

        shifted = {g: rows_at(g) for g in range(-(taps - 1), seg)}
        conv = []
        for g in range(seg):
            acc = cb_ref[:, cols] + shifted[g - (taps - 1)] * cw_ref[pl.ds(0, 1), cols]
            for j in range(1, taps):
                acc = acc + shifted[g - (taps - 1) + j] * cw_ref[pl.ds(j, 1), cols]
            conv.append(acc)
        xc = jnp.concatenate(conv, axis=0)
        ext_ref[c, pl.ds(0, SUBLANE), :] = ext_ref[c, pl.ds(tl, SUBLANE), :]

        xcb = xc.astype(BF16)
        r = jax.nn.sigmoid(jnp.dot(xcb, wa_ref[c], preferred_element_type=F32) + ba_ref[:, cols])
        i = jax.nn.sigmoid(jnp.dot(xcb, wx_ref[c], preferred_element_type=F32) + bx_ref[:, cols])
        log_a = rate[:, cols] * r
        a = jnp.exp(log_a)
        b = jnp.sqrt(1.0 - jnp.exp(2.0 * log_a)) * (i * xc)

        h_loc, prod = [], []
        for g in range(seg):
            rows = slice(g * SUBLANE, (g + 1) * SUBLANE)
            b_g = jnp.where(t * tl + sub * seg + g >= lead, b[rows], 0.0)
            h_loc.append(b_g if g == 0 else a[rows] * h_loc[-1] + b_g)
            prod.append(a[rows] if g == 0 else a[rows] * prod[-1])

        state = [carry_ref[:, cols]]
        for s_ in range(SUBLANE):
            state.append(prod[-1][s_:s_ + 1] * state[-1] + h_loc[-1][s_:s_ + 1])
        carry_ref[:, cols] = state[SUBLANE]
        entering = jnp.concatenate(state[:SUBLANE], axis=0)
        for g in range(seg):
            h_ref[c, pl.ds(g, SUBLANE, stride=seg), :] = h_loc[g] + prod[g] * entering

    gate = gate_ref[0].astype(F32)
    for c in range(n_slab):
        cols = slice(c * LANE, (c + 1) * LANE)
        o_ref[0, :, cols] = (h_ref[c] * jax.nn.gelu(gate[:, cols])).astype(o_ref.dtype)


def lru_branch(z, conv_w, conv_b, w_a, b_a, w_x, b_x, lam, *, batch, rows, lead, width):
    z3 = z.reshape(batch, rows, z.shape[1])
    nblk, bd, _ = w_a.shape
    assert bd == LANE
    tc = _divisor(width, (512, 256, 128))
    tl = _divisor(rows, (352, 96, 32, 8))
    nct = width // tc
    row2 = lambda v: v.reshape(1, width).astype(F32)
    kern = functools.partial(_lru_kernel, lead=lead)
    vec = pl.BlockSpec((1, tc), lambda b, c, t: (0, c))
    return pl.pallas_call(
        kern,
        out_shape=jax.ShapeDtypeStruct((batch, rows, width), BF16),
        grid=(batch, nct, rows // tl),
        in_specs=[pl.BlockSpec((1, tl, tc), lambda b, c, t: (b, t, c)),
                  pl.BlockSpec((1, tl, tc), lambda b, c, t: (b, t, nct + c)),
                  pl.BlockSpec((conv_w.shape[0], tc), lambda b, c, t: (0, c)),
                  vec,
                  pl.BlockSpec((tc // bd, bd, bd), lambda b, c, t: (c, 0, 0)),
                  vec,
                  pl.BlockSpec((tc // bd, bd, bd), lambda b, c, t: (c, 0, 0)),
                  vec, vec],
        out_specs=pl.BlockSpec((1, tl, tc), lambda b, c, t: (b, t, c)),
        scratch_shapes=[pltpu.VMEM((tc // LANE, tl + SUBLANE, LANE), F32),
                        pltpu.VMEM((tc // LANE, tl, LANE), F32), pltpu.VMEM((1, tc), F32)],
        compiler_params=_params("parallel", "parallel", "arbitrary"),
        name="rg_lru",
    )(z3, z3, conv_w.astype(F32), row2(conv_b), w_a.astype(BF16), row2(b_a),
      w_x.astype(BF16), row2(b_x), row2(lam))


def _retention_kernel(lg_ref, q_ref, k_ref, v_ref, g_ref, cos_ref, sin_ref, gain_ref, o_ref,
                      s_ref, d_ref, *, lead, tb, chunk):
    h = pl.program_id(1)
    rows = q_ref.shape[1]
    dk = q_ref.shape[2]
    half = dk // 2
    lg = lg_ref[h]

    ti = lax.broadcasted_iota(jnp.int32, (tb, tb), 0)
    si = lax.broadcasted_iota(jnp.int32, (tb, tb), 1)
    dist = jnp.abs(ti - si).astype(F32)
    d_ref[...] = jnp.where(si // chunk <= ti // chunk, jnp.exp(lg * dist), 0.0)
    idx = lax.broadcasted_iota(jnp.int32, (tb, 1), 0).astype(F32)
    q_decay = jnp.exp(lg * (idx + 1.0))
    k_decay = jnp.exp(lg * (tb - 1.0 - idx))
    block_decay = jnp.exp(lg * jnp.full((1, 1), float(tb), F32))
    s_ref[...] = jnp.zeros_like(s_ref)

    def rotate(x, c, s):
        x1, x2 = x[:, :half], x[:, half:]
        return jnp.concatenate([x1 * c - x2 * s, x1 * s + x2 * c], axis=1)

    def block(n, carry):
        r0 = pl.multiple_of(n * tb, LANE)
        valid = (r0 + lax.broadcasted_iota(jnp.int32, (tb, 1), 0)) >= lead
        c = cos_ref[pl.ds(r0, tb), :]
        s = sin_ref[pl.ds(r0, tb), :]
        q = rotate(q_ref[0, pl.ds(r0, tb), :].astype(F32), c, s)
        k = rotate(k_ref[0, pl.ds(r0, tb), :].astype(F32), c, s) * (dk ** -0.5)
        k = jnp.where(valid, k, 0.0)
        v = v_ref[0, pl.ds(r0, tb), :]
        v = jnp.where(valid, v, jnp.zeros_like(v))

        w = lax.dot_general(q.astype(BF16), k.astype(BF16), (((1,), (1,)), ((), ())),
                            preferred_element_type=F32) * d_ref[...]
        state = s_ref[...]
        o = jnp.dot(w.astype(BF16), v, preferred_element_type=F32)
        o = o + jnp.dot((q * q_decay).astype(BF16), state.astype(BF16),
                        preferred_element_type=F32)
        s_ref[...] = state * block_decay + lax.dot_general(
            (k * k_decay).astype(BF16), v, (((0,), (0,)), ((), ())),
            preferred_element_type=F32)

        ms = jnp.mean(o * o, axis=-1, keepdims=True)
        on = o * lax.rsqrt(ms + RMS_EPS) * gain_ref[0]
        g = g_ref[0, pl.ds(r0, tb), :].astype(F32)
        o_ref[0, pl.ds(r0, tb), :] = (g * jax.nn.sigmoid(g) * on).astype(o_ref.dtype)
        return carry

    lax.fori_loop(0, rows // tb, block, 0)


def retention(z, cos, sin, ret_norm, *, batch, rows, lead, heads):
    cols = z.shape[1]
    z3 = z.reshape(batch, rows, cols)
    vw = ret_norm.shape[0]
    qk = (cols - 2 * vw) // 2
    dk, dv = qk // heads, vw // heads
    tb = _divisor(rows, (384, 128, 64))
    assert tb % RET_CHUNK == 0
    log_gamma = jnp.log(1.0 - 2.0 ** (-5.0 - jnp.arange(heads, dtype=F32)))
    kern = functools.partial(_retention_kernel, lead=lead, tb=tb, chunk=RET_CHUNK)
    kb, vb, gb = qk // dk, 2 * qk // dv, (2 * qk + vw) // dv
    return pl.pallas_call(
        kern,
        out_shape=jax.ShapeDtypeStruct((batch, rows, vw), BF16),
        grid=(batch, heads),
        in_specs=[pl.BlockSpec(memory_space=pltpu.SMEM),
                  pl.BlockSpec((1, rows, dk), lambda b, h: (b, 0, h)),
                  pl.BlockSpec((1, rows, dk), lambda b, h: (b, 0, kb + h)),
                  pl.BlockSpec((1, rows, dv), lambda b, h: (b, 0, vb + h)),
                  pl.BlockSpec((1, rows, dv), lambda b, h: (b, 0, gb + h)),
                  pl.BlockSpec((rows, dk // 2), lambda b, h: (0, 0)),
                  pl.BlockSpec((rows, dk // 2), lambda b, h: (0, 0)),
                  pl.BlockSpec((1, 1, dv), lambda b, h: (h, 0, 0))],
        out_specs=pl.BlockSpec((1, rows, dv), lambda b, h: (b, 0, h)),
        scratch_shapes=[pltpu.VMEM((dk, dv), F32), pltpu.VMEM((tb, tb), F32)],
        compiler_params=_params("parallel", "arbitrary"),
        name="retention",
    )(log_gamma, z3, z3, z3, z3, cos, sin, ret_norm.reshape(heads, 1, dv).astype(F32))


def _ffn(stream, layer, norm, w_gate, w_up, w_down_bf16, last):
    h, hb, ss = stream
    a = norm_swiglu_up(hb, ss, norm, w_gate, w_up, layer)
    return residual_matmul(a, w_down_bf16, layer, h, "ffn_down", emit_norm=not last)


def _lru_fox_layer(stream, geom, j, norm, w_in_all, b_f, conv_w, conv_b, w_a, b_a, w_x, b_x, lam,
                   q_norm, k_norm, w_out_all):
    h, hb, ss = stream
    batch, rows, lead, head_rows = geom
    w_in = w_in_all[j]
    heads = b_f.shape[0]
    head_dim = q_norm.shape[0]
    fox_w = heads * head_dim
    lru_w = lam.shape[0]
    q_col, k_col, v_col, f_col = 2 * lru_w, 2 * lru_w + fox_w, 2 * lru_w + 2 * fox_w, 2 * lru_w + 3 * fox_w

    gains = jnp.ones((1, f_col), F32)
    gains = gains.at[0, q_col:k_col].set(
        jnp.tile(q_norm.astype(F32) * (head_dim ** -0.5 * LOG2E), heads))
    gains = gains.at[0, k_col:v_col].set(jnp.tile(k_norm.astype(F32), heads))
    w_in_bf16 = (w_in[:, :f_col] * norm.astype(F32)[:, None]).astype(BF16)
    z = headnorm_matmul(hb, ss, w_in_bf16, gains, (q_col, v_col), head_dim, "lru_fox_in_proj")

    w_f = jnp.zeros((w_in.shape[0], LANE), F32).at[:, :heads].set(w_in[:, f_col:])
    bias_f = jnp.zeros((1, LANE), F32).at[0, :heads].set(b_f.astype(F32))
    log_f = forget_log_gates(hb, ss, norm, w_f, bias_f).reshape(batch, rows, LANE)
    cum = prefix_sum_time(log_f, lead)

    y_lru = lru_branch(z, conv_w, conv_b, w_a, b_a, w_x, b_x, lam,
                       batch=batch, rows=rows, lead=lead, width=lru_w)
    y_fox = fox_attention(z, cum, batch=batch, rows=rows, lead=lead, head_rows=head_rows,
                          heads=heads, head_dim=head_dim, q_col=q_col, k_col=k_col, v_col=v_col)
    m = batch * rows
    return residual_matmul2(y_lru.reshape(m, lru_w), y_fox.reshape(m, fox_w),
                            w_out_all[j].astype(BF16), h, "lru_fox_out_proj")


def _retention_layer(stream, geom, j, norm, w_in_all, ret_norm, w_out_bf16):
    h, hb, ss = stream
    batch, rows, lead, _ = geom
    vw = ret_norm.shape[0]
    n_in = w_in_all.shape[2]
    dk = (n_in - 2 * vw) // 2 // RET_HEADS
    z = norm_matmul(hb, ss, norm, w_in_all, j, "retention_in_proj")
    half = dk // 2
    inv = ROPE_BASE ** (-jnp.arange(half, dtype=F32) / half)
    pos = (jnp.arange(rows) - lead).astype(F32)
    ang = pos[:, None] * inv[None, :]
    y = retention(z, jnp.cos(ang), jnp.sin(ang), ret_norm, batch=batch, rows=rows, lead=lead,
                  heads=RET_HEADS)
    return residual_matmul(y.reshape(batch * rows, vw), w_out_bf16, j, h, "retention_out_proj")


def kernel(x, meta_tokens, ab_norm, ab_w_in, ab_b_f, ab_conv_w, ab_conv_b, ab_w_a, ab_b_a, ab_w_x, ab_b_x, ab_lambda, ab_q_norm, ab_k_norm, ab_w_out, c_norm, c_w_in, c_ret_norm, c_w_out, ffn_norm, ffn_w_gate, ffn_w_up, ffn_w_down):
    batch, seq, d = x.shape
    n_meta = meta_tokens.shape[0]
    depth = ffn_norm.shape[0]
    lead = (-n_meta) % LANE
    head_rows = lead + n_meta
    rows = head_rows + seq
    assert seq % LANE == 0 and lead % RET_CHUNK == (-n_meta) % RET_CHUNK
    geom = (batch, rows, lead, head_rows)
    m = batch * rows

    head = jnp.concatenate([jnp.zeros((lead, d), F32), meta_tokens.astype(F32)], axis=0)
    stream = tuple(a.reshape(m, a.shape[-1]) for a in embed(x.astype(F32), head))
    w_down_bf16 = ffn_w_down.astype(BF16)
    c_w_out_bf16 = c_w_out.astype(BF16)
    for layer in range(depth):
        j = layer // 2
        if layer % 2 == 0:
            stream = _lru_fox_layer(stream, geom, j, ab_norm[j], ab_w_in, ab_b_f[j], ab_conv_w[j],
                                    ab_conv_b[j], ab_w_a[j], ab_b_a[j], ab_w_x[j], ab_b_x[j],
                                    ab_lambda[j], ab_q_norm[j], ab_k_norm[j], ab_w_out)
        else:
            stream = _retention_layer(stream, geom, j, c_norm[j], c_w_in, c_ret_norm[j], c_w_out_bf16)
        stream = _ffn(stream, layer, ffn_norm[layer], ffn_w_gate, ffn_w_up, w_down_bf16,
                      last=layer == depth - 1)
    return stream[0].reshape(batch, rows, d)[:, head_rows:]
```

```python
import functools

import jax
import jax.numpy as jnp
from jax import lax
from jax.experimental import pallas as pl
from jax.experimental.pallas import tpu as pltpu

F32 = jnp.float32
BF16 = jnp.bfloat16

LANE = 128
SUBLANE = 8
VMEM_LIMIT_BYTES = 60 * 1024 * 1024
VMEM_SLACK_BYTES = 1 * 1024 * 1024

RMS_EPS = 1e-6
LRU_C = 8.0
RET_CHUNK = 64
RET_HEADS = 16
ROPE_BASE = 10000.0
MASK_VALUE = -1e30
LOG2E = 1.4426950408889634


def _divisor(n, candidates):
    for c in candidates:
        if n % c == 0:
            return c
    raise ValueError(f"no tile in {candidates} divides {n}")


def _params(*semantics):
    return pltpu.CompilerParams(dimension_semantics=semantics,
                                vmem_limit_bytes=VMEM_LIMIT_BYTES)


def _lane_partial_sums(sq):
    part = sq[:, :LANE]
    for c in range(1, sq.shape[1] // LANE):
        part = part + sq[:, c * LANE:(c + 1) * LANE]
    return part


def _embed_kernel(head_ref, x_ref, h_ref, hb_ref, ss_ref):
    t = pl.program_id(1)

    def emit(v):
        h_ref[0] = v
        hb_ref[0] = v.astype(BF16)
        ss_ref[0] = _lane_partial_sums(v * v)

    @pl.when(t == 0)
    def _():
        emit(head_ref[...])

    @pl.when(t > 0)
    def _():
        emit(x_ref[0])


def embed(x, head):
    batch, seq, d = x.shape
    hr = head.shape[0]
    rows = hr + seq
    nt = rows // hr
    blk = lambda b, t: (b, t, 0)
    return pl.pallas_call(
        _embed_kernel,
        out_shape=(jax.ShapeDtypeStruct((batch, rows, d), F32),
                   jax.ShapeDtypeStruct((batch, rows, d), BF16),
                   jax.ShapeDtypeStruct((batch, rows, LANE), F32)),
        grid=(batch, nt),
        in_specs=[pl.BlockSpec((hr, d), lambda b, t: (0, 0)),
                  pl.BlockSpec((1, hr, d), lambda b, t: (b, jnp.maximum(t - 1, 0), 0))],
        out_specs=(pl.BlockSpec((1, hr, d), blk), pl.BlockSpec((1, hr, d), blk),
                   pl.BlockSpec((1, hr, LANE), blk)),
        compiler_params=_params("parallel", "arbitrary"),
        name="embed",
    )(head, x)


def _row_scale(ss_ref, d):
    ms = jnp.sum(ss_ref[...], axis=-1, keepdims=True) * (1.0 / d)
    return jnp.broadcast_to(lax.rsqrt(ms + RMS_EPS), ss_ref.shape)


def _scaled(acc, rstd_ref):
    return acc * jnp.tile(rstd_ref[...], (1, acc.shape[1] // LANE))


def _norm_weight(w_ref, g_ref):
    return (w_ref[...] * g_ref[...]).astype(BF16)


_ROW_TILES = (2048, 1536, 1024, 768, 512, 256, 128, 8)


def _fits(double_buffered, resident=0):
    return 2 * sum(double_buffered) + resident + VMEM_SLACK_BYTES <= VMEM_LIMIT_BYTES


def _row_spec(block, index_map, single):
    if single:
        return pl.BlockSpec(block, index_map, pipeline_mode=pl.Buffered(1))
    return pl.BlockSpec(block, index_map)


def _staged_maps(nj, n_steps):
    def product(t):
        u = jnp.maximum(t - 1, 0)
        return u // nj, u % nj
    return (lambda t: (product(t)[0], 0), lambda t: (0, product(t)[1]), product,
            lambda t: (0, jnp.minimum(t, n_steps - 2) % nj))


def _staged_step(t, nj, wb_ref, body):
    @pl.when(t == 0)
    def _():
        wb_ref[...] = jnp.zeros_like(wb_ref)

    @pl.when(t % 2 == 0)
    def _():
        body(0, 1)

    @pl.when(t % 2 == 1)
    def _():
        body(1, 0)


def _first_column(t, nj):
    return jnp.maximum(t - 1, 0) % nj == 0


def _norm_mm_kernel(x_ref, ss_ref, g_ref, w_ref, o_ref, rstd_ref, wb_ref, *, nj):
    t = pl.program_id(0)

    @pl.when(_first_column(t, nj))
    def _():
        rstd_ref[...] = _row_scale(ss_ref, x_ref.shape[1])

    def body(round_slot, use_slot):
        wb_ref[round_slot] = _norm_weight(w_ref, g_ref)
        acc = jnp.dot(x_ref[...], wb_ref[use_slot], preferred_element_type=F32)
        o_ref[...] = _scaled(acc, rstd_ref).astype(o_ref.dtype)

    _staged_step(t, nj, wb_ref, body)


def _headnorm_mm_kernel(x_ref, ss_ref, w_ref, hg_ref, o_ref, rstd_ref, *, norm_lo, norm_hi, head_dim):
    j = pl.program_id(1)

    @pl.when(j == 0)
    def _():
        rstd_ref[...] = _row_scale(ss_ref, x_ref.shape[1])

    acc = _scaled(jnp.dot(x_ref[...], w_ref[...], preferred_element_type=F32), rstd_ref)
    is_norm = jnp.logical_and(j >= norm_lo, j < norm_hi)

    @pl.when(is_norm)
    def _():
        for h in range(acc.shape[1] // head_dim):
            cols = slice(h * head_dim, (h + 1) * head_dim)
            y = acc[:, cols]
            ms = jnp.mean(y * y, axis=-1, keepdims=True)
            o_ref[:, cols] = (y * lax.rsqrt(ms + RMS_EPS) * hg_ref[:, cols]).astype(o_ref.dtype)

    @pl.when(jnp.logical_not(is_norm))
    def _():
        o_ref[...] = acc.astype(o_ref.dtype)


def headnorm_matmul(hb, ss, w, gains, norm_cols, head_dim, name):
    m, k = hb.shape
    n = w.shape[1]
    tm = _divisor(m, _ROW_TILES)
    tn = _divisor(n, (512, 256, 128))
    lo, hi = norm_cols
    assert tn % head_dim == 0 and lo % tn == 0 and hi % tn == 0
    kern = functools.partial(_headnorm_mm_kernel, norm_lo=lo // tn, norm_hi=hi // tn,
                             head_dim=head_dim)
    return pl.pallas_call(
        kern,
        out_shape=jax.ShapeDtypeStruct((m, n), BF16),
        grid=(m // tm, n // tn),
        in_specs=[pl.BlockSpec((tm, k), lambda i, j: (i, 0)),
                  pl.BlockSpec((tm, LANE), lambda i, j: (i, 0)),
                  pl.BlockSpec((k, tn), lambda i, j: (0, j)),
                  pl.BlockSpec((1, tn), lambda i, j: (0, j))],
        out_specs=pl.BlockSpec((tm, tn), lambda i, j: (i, j)),
        scratch_shapes=[pltpu.VMEM((tm, LANE), F32)],
        compiler_params=_params("parallel", "arbitrary"),
        name=name,
    )(hb, ss, w, gains)


def _stacked(w, layer, block, index_map):
    return pl.BlockSpec((None,) + block, lambda *idx: (layer,) + tuple(index_map(*idx)))


def norm_matmul(hb, ss, gain, w, layer, name):
    m, k = hb.shape
    n_out = w.shape[2]
    tm = _divisor(m, _ROW_TILES)
    tn = _divisor(n_out, (512, 256, 128))
    nj = n_out // tn
    n_steps = (m // tm) * nj + 1
    row, _, out, wcol = _staged_maps(nj, n_steps)
    return pl.pallas_call(
        functools.partial(_norm_mm_kernel, nj=nj),
        out_shape=jax.ShapeDtypeStruct((m, n_out), BF16),
        grid=(n_steps,),
        in_specs=[pl.BlockSpec((tm, k), row, pipeline_mode=pl.Buffered(1)),
                  pl.BlockSpec((tm, LANE), row),
                  pl.BlockSpec((k, 1), lambda t: (0, 0)),
                  _stacked(w, layer, (k, tn), wcol)],
        out_specs=pl.BlockSpec((tm, tn), out),
        scratch_shapes=[pltpu.VMEM((tm, LANE), F32), pltpu.VMEM((2, k, tn), BF16)],
        compiler_params=_params("arbitrary"),
        name=name,
    )(hb, ss, gain.reshape(k, 1).astype(F32), w)


def _norm_swiglu_kernel(x_ref, ss_ref, g_ref, wg_ref, wu_ref, o_ref, rstd_ref, wb_ref, *, nj):
    t = pl.program_id(0)

    @pl.when(_first_column(t, nj))
    def _():
        rstd_ref[...] = _row_scale(ss_ref, x_ref.shape[1])

    def body(round_slot, use_slot):
        wb_ref[round_slot, 0] = _norm_weight(wg_ref, g_ref)
        wb_ref[round_slot, 1] = _norm_weight(wu_ref, g_ref)
        x = x_ref[...]
        gate = _scaled(jnp.dot(x, wb_ref[use_slot, 0], preferred_element_type=F32), rstd_ref)
        up = _scaled(jnp.dot(x, wb_ref[use_slot, 1], preferred_element_type=F32), rstd_ref)
        o_ref[...] = (gate * jax.nn.sigmoid(gate) * up).astype(o_ref.dtype)

    _staged_step(t, nj, wb_ref, body)


def norm_swiglu_up(hb, ss, gain, w_gate, w_up, layer):
    m, k = hb.shape
    n = w_gate.shape[2]
    tm = _divisor(m, _ROW_TILES)
    tn = _divisor(n, (256, 128))
    nj = n // tn
    n_steps = (m // tm) * nj + 1
    row, _, out, wcol = _staged_maps(nj, n_steps)
    x_bytes, w_bytes = tm * k * 2, k * tn * 4
    streamed = [2 * w_bytes, tm * tn * 2, tm * LANE * 4, k * LANE * 4]
    resident = tm * LANE * 4 + 2 * w_bytes
    single = not _fits([x_bytes] + streamed, resident)
    return pl.pallas_call(
        functools.partial(_norm_swiglu_kernel, nj=nj),
        out_shape=jax.ShapeDtypeStruct((m, n), BF16),
        grid=(n_steps,),
        in_specs=[_row_spec((tm, k), row, single),
                  pl.BlockSpec((tm, LANE), row),
                  pl.BlockSpec((k, 1), lambda t: (0, 0)),
                  _stacked(w_gate, layer, (k, tn), wcol),
                  _stacked(w_up, layer, (k, tn), wcol)],
        out_specs=pl.BlockSpec((tm, tn), out),
        scratch_shapes=[pltpu.VMEM((tm, LANE), F32), pltpu.VMEM((2, 2, k, tn), BF16)],
        compiler_params=_params("arbitrary"),
        name="swiglu_up",
    )(hb, ss, gain.reshape(k, 1).astype(F32), w_gate, w_up)


def _emit_stream(h_new, first_column, h_ref, hb_ref, ss_ref):
    h_ref[...] = h_new
    hb_ref[...] = h_new.astype(BF16)
    part = _lane_partial_sums(h_new * h_new)

    @pl.when(first_column)
    def _():
        ss_ref[...] = part

    @pl.when(jnp.logical_not(first_column))
    def _():
        ss_ref[...] += part


def _residual_mm_kernel(x_ref, w_ref, r_ref, *out_refs):
    h_new = r_ref[...] + jnp.dot(x_ref[...], w_ref[...], preferred_element_type=F32)
    if len(out_refs) == 1:
        out_refs[0][...] = h_new
    else:
        _emit_stream(h_new, pl.program_id(1) == 0, *out_refs)


def residual_matmul(x, w, layer, res, name, emit_norm=True, drop_head=None):
    k = x.shape[1]
    n = w.shape[2]
    if drop_head is None:
        m = x.shape[0]
        tm = _divisor(m, _ROW_TILES[2:])
        row_of = None
    else:
        batch, rows, head_rows = drop_head
        seq = rows - head_rows
        m = batch * seq
        tm = _divisor(seq, _ROW_TILES[2:])
        per_seq = seq // tm
        row_of = lambda i: pl.multiple_of((i // per_seq) * rows + head_rows + (i % per_seq) * tm,
                                          SUBLANE)
    plans = [(tn, single) for single in (False, True) for tn in (512, 256) if n % tn == 0]
    for tn, single in plans:
        blocks = [k * tn * 2, tm * tn * 4, tm * tn * 4] + ([tm * tn * 2, tm * LANE * 4] if emit_norm else [])
        x_bytes = tm * k * 2
        if _fits(blocks + ([] if single else [x_bytes]), x_bytes if single else 0):
            break
    tile = pl.BlockSpec((tm, tn), lambda i, j: (i, j))
    if row_of is None:
        x_spec = _row_spec((tm, k), lambda i, j: (i, 0), single)
        res_spec = tile
    else:
        x_spec = _row_spec((pl.Element(tm), pl.Element(k)), lambda i, j: (row_of(i), 0), single)
        res_spec = pl.BlockSpec((pl.Element(tm), pl.Element(tn)), lambda i, j: (row_of(i), j * tn))
    out_shape = [jax.ShapeDtypeStruct((m, n), F32)]
    out_specs = [tile]
    if emit_norm:
        out_shape += [jax.ShapeDtypeStruct((m, n), BF16), jax.ShapeDtypeStruct((m, LANE), F32)]
        out_specs += [tile, pl.BlockSpec((tm, LANE), lambda i, j: (i, 0))]
    return pl.pallas_call(
        _residual_mm_kernel,
        out_shape=tuple(out_shape),
        grid=(m // tm, n // tn),
        in_specs=[x_spec, _stacked(w, layer, (k, tn), lambda i, j: (0, j)), res_spec],
        out_specs=tuple(out_specs),
        compiler_params=_params("parallel", "arbitrary"),
        name=name,
    )(x, w, res)


def _residual_mm2_kernel(xa_ref, xb_ref, wa_ref, wb_ref, r_ref, h_ref, hb_ref, ss_ref):
    acc = jnp.dot(xa_ref[...], wa_ref[...], preferred_element_type=F32)
    acc = acc + jnp.dot(xb_ref[...], wb_ref[...], preferred_element_type=F32)
    _emit_stream(r_ref[...] + acc, pl.program_id(1) == 0, h_ref, hb_ref, ss_ref)


def residual_matmul2(xa, xb, w, res, name):
    m, kx = xa.shape
    k, n = w.shape
    assert xb.shape == (m, kx) and k == 2 * kx
    tm = _divisor(m, _ROW_TILES)
    tn = _divisor(n, (512, 256, 128))
    tile = pl.BlockSpec((tm, tn), lambda i, j: (i, j))
    return pl.pallas_call(
        _residual_mm2_kernel,
        out_shape=(jax.ShapeDtypeStruct((m, n), F32), jax.ShapeDtypeStruct((m, n), BF16),
                   jax.ShapeDtypeStruct((m, LANE), F32)),
        grid=(m // tm, n // tn),
        in_specs=[pl.BlockSpec((tm, kx), lambda i, j: (i, 0)),
                  pl.BlockSpec((tm, kx), lambda i, j: (i, 0)),
                  pl.BlockSpec((kx, tn), lambda i, j: (0, j)),
                  pl.BlockSpec((kx, tn), lambda i, j: (1, j)),
                  tile],
        out_specs=(tile, tile, pl.BlockSpec((tm, LANE), lambda i, j: (i, 0))),
        compiler_params=_params("parallel", "arbitrary"),
        name=name,
    )(xa, xb, w, w, res)


def _forget_logit_kernel(x_ref, ss_ref, g_ref, w_ref, b_ref, o_ref):
    acc = jnp.dot(x_ref[...], _norm_weight(w_ref, g_ref), preferred_element_type=F32)
    f = acc * _row_scale(ss_ref, x_ref.shape[1]) + b_ref[...]
    o_ref[...] = jnp.minimum(f, 0.0) - jnp.log1p(jnp.exp(-jnp.abs(f)))


def forget_log_gates(hb, ss, gain, w_f, b_f):
    m, k = hb.shape
    n = w_f.shape[1]
    assert n == LANE
    tm = _divisor(m, _ROW_TILES)
    return pl.pallas_call(
        _forget_logit_kernel,
        out_shape=jax.ShapeDtypeStruct((m, n), F32),
        grid=(m // tm,),
        in_specs=[pl.BlockSpec((tm, k), lambda i: (i, 0)),
                  pl.BlockSpec((tm, LANE), lambda i: (i, 0)),
                  pl.BlockSpec((k, 1), lambda i: (0, 0)),
                  pl.BlockSpec((k, n), lambda i: (0, 0)),
                  pl.BlockSpec((1, n), lambda i: (0, 0))],
        out_specs=pl.BlockSpec((tm, n), lambda i: (i, 0)),
        compiler_params=_params("parallel"),
        name="fox_forget_gates",
    )(hb, ss, gain.reshape(k, 1).astype(F32), w_f, b_f)


def _prefix_sum_kernel(x_ref, o_ref, *, lead):
    x = x_ref[0]
    rows = x.shape[0]
    row = lax.broadcasted_iota(jnp.int32, x.shape, 0)
    x = jnp.where(row >= lead, x, 0.0)
    shift = 1
    while shift < rows:
        x = x + jnp.where(row >= shift, pltpu.roll(x, shift, 0), 0.0)
        shift *= 2
    o_ref[0] = x


def prefix_sum_time(x, lead):
    b, rows, lanes = x.shape
    return pl.pallas_call(
        functools.partial(_prefix_sum_kernel, lead=lead),
        out_shape=jax.ShapeDtypeStruct(x.shape, F32),
        grid=(b,),
        in_specs=[pl.BlockSpec((1, rows, lanes), lambda i: (i, 0, 0))],
        out_specs=pl.BlockSpec((1, rows, lanes), lambda i: (i, 0, 0)),
        compiler_params=_params("parallel"),
        name="fox_prefix_sum",
    )(x)


def _split3(x):
    hi = x.astype(BF16).astype(F32)
    rem = x - hi
    mid = rem.astype(BF16).astype(F32)
    lo = (rem - mid).astype(BF16).astype(F32)
    return hi, mid, lo


def _fox_kernel(q_ref, k_ref, v_ref, cum_ref, o_ref, qb_ref, kb_ref, m_ref, acc_ref,
                *, lead, head_rows, tq):
    h = pl.program_id(1)
    rows, dh = q_ref.shape[1], q_ref.shape[2]
    n_main = (rows - head_rows) // tq
    lanes = cum_ref.shape[2]

    def build_bias(r0, n):
        lane = lax.broadcasted_iota(jnp.int32, (n, lanes), 1)
        c = jnp.sum(jnp.where(lane == h, cum_ref[0, pl.ds(r0, n), :], 0.0),
                    axis=-1, keepdims=True) * LOG2E
        hi, mid, lo = _split3(c)
        is_pad = (r0 + lax.broadcasted_iota(jnp.int32, (n, 1), 0)) < lead
        qb = jnp.where(lane == 0, hi, jnp.where(lane == 1, mid, jnp.where(lane == 2, lo,
             jnp.where(lane < 7, 1.0, 0.0))))
        kb = jnp.where(lane < 3, 1.0, jnp.where(lane == 3, -hi, jnp.where(lane == 4, -mid,
             jnp.where(lane == 5, -lo, jnp.where(jnp.logical_and(lane == 6, is_pad),
                                                  MASK_VALUE, 0.0)))))
        qb_ref[pl.ds(r0, n), :] = qb.astype(BF16)
        kb_ref[pl.ds(r0, n), :] = kb.astype(BF16)

    build_bias(0, head_rows)
    for i in range(n_main):
        build_bias(head_rows + i * tq, tq)

    def row_max(s):
        return jnp.broadcast_to(jnp.max(s, axis=-1, keepdims=True), (s.shape[0], lanes))

    def scores(q_aug, k0, nk):
        k_aug = jnp.concatenate([k_ref[0, pl.ds(k0, nk), :], kb_ref[pl.ds(k0, nk), :]], axis=1)
        return lax.dot_general(q_aug, k_aug, (((1,), (1,)), ((), ())),
                               preferred_element_type=F32)

    def values(k0, nk):
        return jnp.concatenate([v_ref[0, pl.ds(k0, nk), :], jnp.ones((nk, dh), BF16)], axis=1)

    def accumulate(s, s_max, v_aug):
        nq, nk = s.shape
        m_prev = m_ref[pl.ds(0, nq), :]
        m_new = jnp.maximum(m_prev, s_max)
        alpha = jnp.exp2(m_prev - m_new)
        p = jnp.exp2(s - jnp.tile(m_new, (1, nk // lanes)))
        acc_ref[pl.ds(0, nq), :] = (jnp.tile(alpha, (1, 2)) * acc_ref[pl.ds(0, nq), :]
                                    + jnp.dot(p.astype(BF16), v_aug, preferred_element_type=F32))
        m_ref[pl.ds(0, nq), :] = m_new

    def causal(s):
        return jnp.where(lax.broadcasted_iota(jnp.int32, s.shape, 1)
                         <= lax.broadcasted_iota(jnp.int32, s.shape, 0), s, MASK_VALUE)

    def reset(nq):
        m_ref[pl.ds(0, nq), :] = jnp.full((nq, lanes), MASK_VALUE, F32)
        acc_ref[pl.ds(0, nq), :] = jnp.zeros((nq, 2 * dh), F32)

    def finish(r0, nq):
        o_ref[0, pl.ds(r0, nq), :] = (acc_ref[pl.ds(0, nq), pl.ds(0, dh)]
                                      / acc_ref[pl.ds(0, nq), pl.ds(dh, dh)]).astype(o_ref.dtype)

    def q_rows(r0, nq):
        return jnp.concatenate([q_ref[0, pl.ds(r0, nq), :], qb_ref[pl.ds(r0, nq), :]], axis=1)

    reset(head_rows)
    s_head = causal(scores(q_rows(0, head_rows), 0, head_rows))
    accumulate(s_head, row_max(s_head), values(0, head_rows))
    finish(0, head_rows)

    def q_tile(i, carry):
        r0 = pl.multiple_of(head_rows + i * tq, LANE)
        q_aug = q_rows(r0, tq)
        reset(tq)

        def kv_tile(j, s):
            k0 = pl.multiple_of(head_rows + j * tq, LANE)
            s_next = scores(q_aug, pl.multiple_of(k0 + tq, LANE), tq)
            accumulate(s, row_max(s), values(k0, tq))
            return s_next

        s_diag = lax.fori_loop(0, i, kv_tile, scores(q_aug, head_rows, tq))
        s_last = jnp.concatenate([causal(s_diag), scores(q_aug, 0, head_rows)], axis=1)
        v_last = jnp.concatenate([values(r0, tq), values(0, head_rows)], axis=0)
        accumulate(s_last, row_max(s_last), v_last)
        finish(r0, tq)
        return carry

    lax.fori_loop(0, n_main, q_tile, 0)


def fox_attention(z, cum, *, batch, rows, lead, head_rows, heads, head_dim, q_col, k_col, v_col):
    z3 = z.reshape(batch, rows, z.shape[1])
    lanes = cum.shape[2]
    assert head_dim == lanes and head_rows == lanes
    tq = _divisor(rows - head_rows, (512, 256, 128))
    qb, kb, vb = q_col // head_dim, k_col // head_dim, v_col // head_dim
    kern = functools.partial(_fox_kernel, lead=lead, head_rows=head_rows, tq=tq)
    tmax = max(tq, head_rows)
    return pl.pallas_call(
        kern,
        out_shape=jax.ShapeDtypeStruct((batch, rows, heads * head_dim), BF16),
        grid=(batch, heads),
        in_specs=[pl.BlockSpec((1, rows, head_dim), lambda b, h: (b, 0, qb + h)),
                  pl.BlockSpec((1, rows, head_dim), lambda b, h: (b, 0, kb + h)),
                  pl.BlockSpec((1, rows, head_dim), lambda b, h: (b, 0, vb + h)),
                  pl.BlockSpec((1, rows, lanes), lambda b, h: (b, 0, 0))],
        out_specs=pl.BlockSpec((1, rows, head_dim), lambda b, h: (b, 0, h)),
        scratch_shapes=[pltpu.VMEM((rows, lanes), BF16), pltpu.VMEM((rows, lanes), BF16),
                        pltpu.VMEM((tmax, lanes), F32), pltpu.VMEM((tmax, 2 * head_dim), F32)],
        compiler_params=_params("parallel", "arbitrary"),
        name="fox_attention",
    )(z3, z3, z3, cum)


def _lru_kernel(x_ref, gate_ref, cw_ref, cb_ref, wa_ref, ba_ref, wx_ref, bx_ref, lam_ref,
                o_ref, ext_ref, h_ref, carry_ref, *, lead):
    t = pl.program_id(2)
    tl, tc = x_ref.shape[1], x_ref.shape[2]
    taps = cw_ref.shape[0]
    seg = tl // SUBLANE
    n_slab = tc // LANE
    top = min(tl, -(-lead // SUBLANE) * SUBLANE)

    @pl.when(t == 0)
    def _():
        ext_ref[:, pl.ds(0, SUBLANE), :] = jnp.zeros((n_slab, SUBLANE, LANE), F32)
        carry_ref[...] = jnp.zeros_like(carry_ref)

    x = x_ref[0].astype(F32)
    row_top = t * tl + lax.broadcasted_iota(jnp.int32, (top, 1), 0)
    x = jnp.concatenate([jnp.where(row_top >= lead, x[:top], 0.0), x[top:]], axis=0)
    lam = lam_ref[...]
    rate = -LRU_C * (jnp.maximum(-lam, 0.0) + jnp.log1p(jnp.exp(-jnp.abs(lam))))
    sub = lax.broadcasted_iota(jnp.int32, (SUBLANE, 1), 0)

    for c in range(n_slab):
        cols = slice(c * LANE, (c + 1) * LANE)
        ext_ref[c, pl.ds(SUBLANE, tl), :] = x[:, cols]

        def rows_at(g):
            return ext_ref[c, pl.ds(SUBLANE + g, SUBLANE, stride=seg), :]

        shifted = {g: rows_at(g) for g in range(-(taps - 1), seg)}
        conv = []
        for g in range(seg):
            acc = cb_ref[:, cols] + shifted[g - (taps - 1)] * cw_ref[pl.ds(0, 1), cols]
            for j in range(1, taps):
                acc = acc + shifted[g - (taps - 1) + j] * cw_ref[pl.ds(j, 1), cols]
            conv.append(acc)
        xc = jnp.concatenate(conv, axis=0)
        ext_ref[c, pl.ds(0, SUBLANE), :] = ext_ref[c, pl.ds(tl, SUBLANE), :]

        xcb = xc.astype(BF16)
        r = jax.nn.sigmoid(jnp.dot(xcb, wa_ref[c], preferred_element_type=F32) + ba_ref[:, cols])
        i = jax.nn.sigmoid(jnp.dot(xcb, wx_ref[c], preferred_element_type=F32) + bx_ref[:, cols])
        log_a = rate[:, cols] * r
        a = jnp.exp(log_a)
        b = jnp.sqrt(1.0 - jnp.exp(2.0 * log_a)) * (i * xc)

        h_loc, prod = [], []
        for g in range(seg):
            rows = slice(g * SUBLANE, (g + 1) * SUBLANE)
            b_g = jnp.where(t * tl + sub * seg + g >= lead, b[rows], 0.0)
            h_loc.append(b_g if g == 0 else a[rows] * h_loc[-1] + b_g)
            prod.append(a[rows] if g == 0 else a[rows] * prod[-1])

        state = [carry_ref[:, cols]]
        for s_ in range(SUBLANE):
            state.append(prod[-1][s_:s_ + 1] * state[-1] + h_loc[-1][s_:s_ + 1])
        carry_ref[:, cols] = state[SUBLANE]
        entering = jnp.concatenate(state[:SUBLANE], axis=0)
        for g in range(seg):
            h_ref[c, pl.ds(g, SUBLANE, stride=seg), :] = h_loc[g] + prod[g] * entering

    gate = gate_ref[0].astype(F32)
    for c in range(n_slab):
        cols = slice(c * LANE, (c + 1) * LANE)
        o_ref[0, :, cols] = (h_ref[c] * jax.nn.gelu(gate[:, cols])).astype(o_ref.dtype)


def lru_branch(z, conv_w, conv_b, w_a, b_a, w_x, b_x, lam, *, batch, rows, lead, width):
    z3 = z.reshape(batch, rows, z.shape[1])
    nblk, bd, _ = w_a.shape
    assert bd == LANE
    tc = _divisor(width, (512, 256, 128))
    tl = _divisor(rows, (352, 96, 32, 8))
    nct = width // tc
    row2 = lambda v: v.reshape(1, width).astype(F32)
    kern = functools.partial(_lru_kernel, lead=lead)
    vec = pl.BlockSpec((1, tc), lambda b, c, t: (0, c))
    return pl.pallas_call(
        kern,
        out_shape=jax.ShapeDtypeStruct((batch, rows, width), BF16),
        grid=(batch, nct, rows // tl),
        in_specs=[pl.BlockSpec((1, tl, tc), lambda b, c, t: (b, t, c)),
                  pl.BlockSpec((1, tl, tc), lambda b, c, t: (b, t, nct + c)),
                  pl.BlockSpec((conv_w.shape[0], tc), lambda b, c, t: (0, c)),
                  vec,
                  pl.BlockSpec((tc // bd, bd, bd), lambda b, c, t: (c, 0, 0)),
                  vec,
                  pl.BlockSpec((tc // bd, bd, bd), lambda b, c, t: (c, 0, 0)),
                  vec, vec],
        out_specs=pl.BlockSpec((1, tl, tc), lambda b, c, t: (b, t, c)),
        scratch_shapes=[pltpu.VMEM((tc // LANE, tl + SUBLANE, LANE), F32),
                        pltpu.VMEM((tc // LANE, tl, LANE), F32), pltpu.VMEM((1, tc), F32)],
        compiler_params=_params("parallel", "parallel", "arbitrary"),
        name="rg_lru",
    )(z3, z3, conv_w.astype(F32), row2(conv_b), w_a.astype(BF16), row2(b_a),
      w_x.astype(BF16), row2(b_x), row2(lam))


def _retention_kernel(lg_ref, q_ref, k_ref, v_ref, g_ref, cos_ref, sin_ref, gain_ref, o_ref,
                      s_ref, d_ref, *, lead, tb, chunk):
    h = pl.program_id(1)
    rows = q_ref.shape[1]
    dk = q_ref.shape[2]
    half = dk // 2
    lg = lg_ref[h]

    ti = lax.broadcasted_iota(jnp.int32, (tb, tb), 0)
    si = lax.broadcasted_iota(jnp.int32, (tb, tb), 1)
    dist = jnp.abs(ti - si).astype(F32)
    d_ref[...] = jnp.where(si // chunk <= ti // chunk, jnp.exp(lg * dist), 0.0)
    idx = lax.broadcasted_iota(jnp.int32, (tb, 1), 0).astype(F32)
    q_decay = jnp.exp(lg * (idx + 1.0))
    k_decay = jnp.exp(lg * (tb - 1.0 - idx))
    block_decay = jnp.exp(lg * jnp.full((1, 1), float(tb), F32))
    s_ref[...] = jnp.zeros_like(s_ref)

    def rotate(x, c, s):
        x1, x2 = x[:, :half], x[:, half:]
        return jnp.concatenate([x1 * c - x2 * s, x1 * s + x2 * c], axis=1)

    def block(n, carry):
        r0 = pl.multiple_of(n * tb, LANE)
        valid = (r0 + lax.broadcasted_iota(jnp.int32, (tb, 1), 0)) >= lead
        c = cos_ref[pl.ds(r0, tb), :]
        s = sin_ref[pl.ds(r0, tb), :]
        q = rotate(q_ref[0, pl.ds(r0, tb), :].astype(F32), c, s)
        k = rotate(k_ref[0, pl.ds(r0, tb), :].astype(F32), c, s) * (dk ** -0.5)
        k = jnp.where(valid, k, 0.0)
        v = v_ref[0, pl.ds(r0, tb), :]
        v = jnp.where(valid, v, jnp.zeros_like(v))

        w = lax.dot_general(q.astype(BF16), k.astype(BF16), (((1,), (1,)), ((), ())),
                            preferred_element_type=F32) * d_ref[...]
        state = s_ref[...]
        o = jnp.dot(w.astype(BF16), v, preferred_element_type=F32)
        o = o + jnp.dot((q * q_decay).astype(BF16), state.astype(BF16),
                        preferred_element_type=F32)
        s_ref[...] = state * block_decay + lax.dot_general(
            (k * k_decay).astype(BF16), v, (((0,), (0,)), ((), ())),
            preferred_element_type=F32)

        ms = jnp.mean(o * o, axis=-1, keepdims=True)
        on = o * lax.rsqrt(ms + RMS_EPS) * gain_ref[0]
        g = g_ref[0, pl.ds(r0, tb), :].astype(F32)
        o_ref[0, pl.ds(r0, tb), :] = (g * jax.nn.sigmoid(g) * on).astype(o_ref.dtype)
        return carry

    lax.fori_loop(0, rows // tb, block, 0)


def retention(z, cos, sin, ret_norm, *, batch, rows, lead, heads):
    cols = z.shape[1]
    z3 = z.reshape(batch, rows, cols)
    vw = ret_norm.shape[0]
    qk = (cols - 2 * vw) // 2
    dk, dv = qk // heads, vw // heads
    tb = _divisor(rows, (384, 128, 64))
    assert tb % RET_CHUNK == 0
    log_gamma = jnp.log(1.0 - 2.0 ** (-5.0 - jnp.arange(heads, dtype=F32)))
    kern = functools.partial(_retention_kernel, lead=lead, tb=tb, chunk=RET_CHUNK)
    kb, vb, gb = qk // dk, 2 * qk // dv, (2 * qk + vw) // dv
    return pl.pallas_call(
        kern,
        out_shape=jax.ShapeDtypeStruct((batch, rows, vw), BF16),
        grid=(batch, heads),
        in_specs=[pl.BlockSpec(memory_space=pltpu.SMEM),
                  pl.BlockSpec((1, rows, dk), lambda b, h: (b, 0, h)),
                  pl.BlockSpec((1, rows, dk), lambda b, h: (b, 0, kb + h)),
                  pl.BlockSpec((1, rows, dv), lambda b, h: (b, 0, vb + h)),
                  pl.BlockSpec((1, rows, dv), lambda b, h: (b, 0, gb + h)),
                  pl.BlockSpec((rows, dk // 2), lambda b, h: (0, 0)),
                  pl.BlockSpec((rows, dk // 2), lambda b, h: (0, 0)),
                  pl.BlockSpec((1, 1, dv), lambda b, h: (h, 0, 0))],
        out_specs=pl.BlockSpec((1, rows, dv), lambda b, h: (b, 0, h)),
        scratch_shapes=[pltpu.VMEM((dk, dv), F32), pltpu.VMEM((tb, tb), F32)],
        compiler_params=_params("parallel", "arbitrary"),
        name="retention",
    )(log_gamma, z3, z3, z3, z3, cos, sin, ret_norm.reshape(heads, 1, dv).astype(F32))


def _ffn(stream, layer, norm, w_gate, w_up, w_down_bf16, last):
    h, hb, ss = stream
    a = norm_swiglu_up(hb, ss, norm, w_gate, w_up, layer)
    return residual_matmul(a, w_down_bf16, layer, h, "ffn_down", emit_norm=not last)


def _lru_fox_layer(stream, geom, j, norm, w_in_all, b_f, conv_w, conv_b, w_a, b_a, w_x, b_x, lam,
                   q_norm, k_norm, w_out_all):
    h, hb, ss = stream
    batch, rows, lead, head_rows = geom
    w_in = w_in_all[j]
    heads = b_f.shape[0]
    head_dim = q_norm.shape[0]
    fox_w = heads * head_dim
    lru_w = lam.shape[0]
    q_col, k_col, v_col, f_col = 2 * lru_w, 2 * lru_w + fox_w, 2 * lru_w + 2 * fox_w, 2 * lru_w + 3 * fox_w

    gains = jnp.ones((1, f_col), F32)
    gains = gains.at[0, q_col:k_col].set(
        jnp.tile(q_norm.astype(F32) * (head_dim ** -0.5 * LOG2E), heads))
    gains = gains.at[0, k_col:v_col].set(jnp.tile(k_norm.astype(F32), heads))
    w_in_bf16 = (w_in[:, :f_col] * norm.astype(F32)[:, None]).astype(BF16)
    z = headnorm_matmul(hb, ss, w_in_bf16, gains, (q_col, v_col), head_dim, "lru_fox_in_proj")

    w_f = jnp.zeros((w_in.shape[0], LANE), F32).at[:, :heads].set(w_in[:, f_col:])
    bias_f = jnp.zeros((1, LANE), F32).at[0, :heads].set(b_f.astype(F32))
    log_f = forget_log_gates(hb, ss, norm, w_f, bias_f).reshape(batch, rows, LANE)
    cum = prefix_sum_time(log_f, lead)

    y_lru = lru_branch(z, conv_w, conv_b, w_a, b_a, w_x, b_x, lam,
                       batch=batch, rows=rows, lead=lead, width=lru_w)
    y_fox = fox_attention(z, cum, batch=batch, rows=rows, lead=lead, head_rows=head_rows,
                          heads=heads, head_dim=head_dim, q_col=q_col, k_col=k_col, v_col=v_col)
    m = batch * rows
    return residual_matmul2(y_lru.reshape(m, lru_w), y_fox.reshape(m, fox_w),
                            w_out_all[j].astype(BF16), h, "lru_fox_out_proj")


def _retention_layer(stream, geom, j, norm, w_in_all, ret_norm, w_out_bf16, last):
    h, hb, ss = stream
    batch, rows, lead, _ = geom
    vw = ret_norm.shape[0]
    n_in = w_in_all.shape[2]
    dk = (n_in - 2 * vw) // 2 // RET_HEADS
    z = norm_matmul(hb, ss, norm, w_in_all, j, "retention_in_proj")
    half = dk // 2
    inv = ROPE_BASE ** (-jnp.arange(half, dtype=F32) / half)
    pos = (jnp.arange(rows) - lead).astype(F32)
    ang = pos[:, None] * inv[None, :]
    y = retention(z, jnp.cos(ang), jnp.sin(ang), ret_norm, batch=batch, rows=rows, lead=lead,
                  heads=RET_HEADS)
    return residual_matmul(y.reshape(batch * rows, vw), w_out_bf16, j, h, "retention_out_proj",
                           drop_head=geom[:2] + geom[3:] if last else None)


def kernel(x, meta_tokens, ab_norm, ab_w_in, ab_b_f, ab_conv_w, ab_conv_b, ab_w_a, ab_b_a, ab_w_x, ab_b_x, ab_lambda, ab_q_norm, ab_k_norm, ab_w_out, c_norm, c_w_in, c_ret_norm, c_w_out, ffn_norm, ffn_w_gate, ffn_w_up, ffn_w_down):
    batch, seq, d = x.shape
    n_meta = meta_tokens.shape[0]
    depth = ffn_norm.shape[0]
    lead = (-n_meta) % LANE
    head_rows = lead + n_meta
    rows = head_rows + seq
    assert seq % LANE == 0 and lead % RET_CHUNK == (-n_meta) % RET_CHUNK
    geom = (batch, rows, lead, head_rows)
    m = batch * rows

    head = jnp.concatenate([jnp.zeros((lead, d), F32), meta_tokens.astype(F32)], axis=0)
    stream = tuple(a.reshape(m, a.shape[-1]) for a in embed(x.astype(F32), head))
    w_down_bf16 = ffn_w_down.astype(BF16)
    c_w_out_bf16 = c_w_out.astype(BF16)
    for layer in range(depth):
        j = layer // 2
        if layer % 2 == 0:
            stream = _lru_fox_layer(stream, geom, j, ab_norm[j], ab_w_in, ab_b_f[j], ab_conv_w[j],
                                    ab_conv_b[j], ab_w_a[j], ab_b_a[j], ab_w_x[j], ab_b_x[j],
                                    ab_lambda[j], ab_q_norm[j], ab_k_norm[j], ab_w_out)
        else:
            stream = _retention_layer(stream, geom, j, c_norm[j], c_w_in, c_ret_norm[j], c_w_out_bf16,
                                      last=layer == depth - 1)
        stream = _ffn(stream, layer, ffn_norm[layer], ffn_w_gate, ffn_w_up, w_down_bf16,
                      last=layer == depth - 1)
    h = stream[0]
    if h.shape[0] == batch * seq:
        return h.reshape(batch, seq, d)
    return h.reshape(batch, rows, d)[:, head_rows:]
```

```python
import functools

import jax
import jax.numpy as jnp
from jax import lax
from jax.experimental import pallas as pl
from jax.experimental.pallas import tpu as pltpu

F32 = jnp.float32
BF16 = jnp.bfloat16

LANE = 128
SUBLANE = 8
VMEM_LIMIT_BYTES = 60 * 1024 * 1024
VMEM_SLACK_BYTES = 1 * 1024 * 1024

RMS_EPS = 1e-6
LRU_C = 8.0
RET_CHUNK = 64
RET_HEADS = 16
ROPE_BASE = 10000.0
MASK_VALUE = -1e30
LOG2E = 1.4426950408889634


def _divisor(n, candidates):
    for c in candidates:
        if n % c == 0:
            return c
    raise ValueError(f"no tile in {candidates} divides {n}")


def _params(*semantics):
    return pltpu.CompilerParams(dimension_semantics=semantics,
                                vmem_limit_bytes=VMEM_LIMIT_BYTES)


def _lane_partial_sums(sq):
    part = sq[:, :LANE]
    for c in range(1, sq.shape[1] // LANE):
        part = part + sq[:, c * LANE:(c + 1) * LANE]
    return part


def _embed_kernel(head_ref, x_ref, h_ref, hb_ref, ss_ref):
    t = pl.program_id(1)

    def emit(v):
        h_ref[0] = v
        hb_ref[0] = v.astype(BF16)
        ss_ref[0] = _lane_partial_sums(v * v)

    @pl.when(t == 0)
    def _():
        emit(head_ref[...])

    @pl.when(t > 0)
    def _():
        emit(x_ref[0])


def embed(x, head):
    batch, seq, d = x.shape
    hr = head.shape[0]
    rows = hr + seq
    nt = rows // hr
    blk = lambda b, t: (b, t, 0)
    return pl.pallas_call(
        _embed_kernel,
        out_shape=(jax.ShapeDtypeStruct((batch, rows, d), F32),
                   jax.ShapeDtypeStruct((batch, rows, d), BF16),
                   jax.ShapeDtypeStruct((batch, rows, LANE), F32)),
        grid=(batch, nt),
        in_specs=[pl.BlockSpec((hr, d), lambda b, t: (0, 0)),
                  pl.BlockSpec((1, hr, d), lambda b, t: (b, jnp.maximum(t - 1, 0), 0))],
        out_specs=(pl.BlockSpec((1, hr, d), blk), pl.BlockSpec((1, hr, d), blk),
                   pl.BlockSpec((1, hr, LANE), blk)),
        compiler_params=_params("parallel", "arbitrary"),
        name="embed",
    )(head, x)


def _row_scale(ss_ref, d):
    ms = jnp.sum(ss_ref[...], axis=-1, keepdims=True) * (1.0 / d)
    return jnp.broadcast_to(lax.rsqrt(ms + RMS_EPS), ss_ref.shape)


def _scaled(acc, rstd_ref):
    return acc * jnp.tile(rstd_ref[...], (1, acc.shape[1] // LANE))


def _norm_weight(w_ref, g_ref):
    return (w_ref[...] * g_ref[...]).astype(BF16)


_ROW_TILES = (2048, 1536, 1024, 768, 512, 256, 128, 8)


def _fits(double_buffered, resident=0):
    return 2 * sum(double_buffered) + resident + VMEM_SLACK_BYTES <= VMEM_LIMIT_BYTES


def _row_spec(block, index_map, single):
    if single:
        return pl.BlockSpec(block, index_map, pipeline_mode=pl.Buffered(1))
    return pl.BlockSpec(block, index_map)


def _staged_maps(nj, n_steps):
    def product(t):
        u = jnp.maximum(t - 1, 0)
        return u // nj, u % nj
    return (lambda t: (product(t)[0], 0), lambda t: (0, product(t)[1]), product,
            lambda t: (0, jnp.minimum(t, n_steps - 2) % nj))


def _staged_step(t, nj, wb_ref, body):
    @pl.when(t == 0)
    def _():
        wb_ref[...] = jnp.zeros_like(wb_ref)

    @pl.when(t % 2 == 0)
    def _():
        body(0, 1)

    @pl.when(t % 2 == 1)
    def _():
        body(1, 0)


_RIDER_BLOCK = (256, 512)


def _rider_specs(w, layer, n_steps):
    _, r, c = w.shape
    br, bc = _RIDER_BLOCK
    assert r % br == 0 and c % bc == 0
    ncb = c // bc
    nb = (r // br) * ncb
    assert nb <= n_steps
    def block(t):
        u = jnp.minimum(t, nb - 1)
        return u // ncb, u % ncb
    return _stacked(w, layer, (br, bc), block), pl.BlockSpec((br, bc), block)


def _first_column(t, nj):
    return jnp.maximum(t - 1, 0) % nj == 0


def _norm_mm_kernel(x_ref, ss_ref, g_ref, w_ref, side_ref, o_ref, side_out_ref, rstd_ref, wb_ref,
                    *, nj):
    t = pl.program_id(0)

    @pl.when(_first_column(t, nj))
    def _():
        rstd_ref[...] = _row_scale(ss_ref, x_ref.shape[1])

    def body(round_slot, use_slot):
        wb_ref[round_slot] = _norm_weight(w_ref, g_ref)
        side_out_ref[...] = side_ref[...].astype(BF16)
        acc = jnp.dot(x_ref[...], wb_ref[use_slot], preferred_element_type=F32)
        o_ref[...] = _scaled(acc, rstd_ref).astype(o_ref.dtype)

    _staged_step(t, nj, wb_ref, body)


def _headnorm_mm_kernel(x_ref, ss_ref, w_ref, hg_ref, o_ref, rstd_ref, *, norm_lo, norm_hi, head_dim):
    j = pl.program_id(1)

    @pl.when(j == 0)
    def _():
        rstd_ref[...] = _row_scale(ss_ref, x_ref.shape[1])

    acc = _scaled(jnp.dot(x_ref[...], w_ref[...], preferred_element_type=F32), rstd_ref)
    is_norm = jnp.logical_and(j >= norm_lo, j < norm_hi)

    @pl.when(is_norm)
    def _():
        for h in range(acc.shape[1] // head_dim):
            cols = slice(h * head_dim, (h + 1) * head_dim)
            y = acc[:, cols]
            ms = jnp.mean(y * y, axis=-1, keepdims=True)
            o_ref[:, cols] = (y * lax.rsqrt(ms + RMS_EPS) * hg_ref[:, cols]).astype(o_ref.dtype)

    @pl.when(jnp.logical_not(is_norm))
    def _():
        o_ref[...] = acc.astype(o_ref.dtype)


def headnorm_matmul(hb, ss, w, gains, norm_cols, head_dim, name):
    m, k = hb.shape
    n = w.shape[1]
    tm = _divisor(m, _ROW_TILES)
    tn = _divisor(n, (512, 256, 128))
    lo, hi = norm_cols
    assert tn % head_dim == 0 and lo % tn == 0 and hi % tn == 0
    kern = functools.partial(_headnorm_mm_kernel, norm_lo=lo // tn, norm_hi=hi // tn,
                             head_dim=head_dim)
    return pl.pallas_call(
        kern,
        out_shape=jax.ShapeDtypeStruct((m, n), BF16),
        grid=(m // tm, n // tn),
        in_specs=[pl.BlockSpec((tm, k), lambda i, j: (i, 0)),
                  pl.BlockSpec((tm, LANE), lambda i, j: (i, 0)),
                  pl.BlockSpec((k, tn), lambda i, j: (0, j)),
                  pl.BlockSpec((1, tn), lambda i, j: (0, j))],
        out_specs=pl.BlockSpec((tm, tn), lambda i, j: (i, j)),
        scratch_shapes=[pltpu.VMEM((tm, LANE), F32)],
        compiler_params=_params("parallel", "arbitrary"),
        name=name,
    )(hb, ss, w, gains)


def _stacked(w, layer, block, index_map):
    return pl.BlockSpec((None,) + block, lambda *idx: (layer,) + tuple(index_map(*idx)))


def norm_matmul(hb, ss, gain, w, layer, side, name):
    m, k = hb.shape
    n_out = w.shape[2]
    tm = _divisor(m, _ROW_TILES)
    tn = _divisor(n_out, (512, 256, 128))
    nj = n_out // tn
    n_steps = (m // tm) * nj + 1
    row, _, out, wcol = _staged_maps(nj, n_steps)
    side_in, side_out = _rider_specs(side, layer, n_steps)
    return pl.pallas_call(
        functools.partial(_norm_mm_kernel, nj=nj),
        out_shape=(jax.ShapeDtypeStruct((m, n_out), BF16),
                   jax.ShapeDtypeStruct(side.shape[1:], BF16)),
        grid=(n_steps,),
        in_specs=[pl.BlockSpec((tm, k), row, pipeline_mode=pl.Buffered(1)),
                  pl.BlockSpec((tm, LANE), row),
                  pl.BlockSpec((k, 1), lambda t: (0, 0)),
                  _stacked(w, layer, (k, tn), wcol),
                  side_in],
        out_specs=(pl.BlockSpec((tm, tn), out), side_out),
        scratch_shapes=[pltpu.VMEM((tm, LANE), F32), pltpu.VMEM((2, k, tn), BF16)],
        compiler_params=_params("arbitrary"),
        name=name,
    )(hb, ss, gain.reshape(k, 1).astype(F32), w, side)


def _norm_swiglu_kernel(x_ref, ss_ref, g_ref, wg_ref, wu_ref, side_ref, o_ref, side_out_ref,
                        rstd_ref, wb_ref, *, nj):
    t = pl.program_id(0)

    @pl.when(_first_column(t, nj))
    def _():
        rstd_ref[...] = _row_scale(ss_ref, x_ref.shape[1])

    def body(round_slot, use_slot):
        wb_ref[round_slot, 0] = _norm_weight(wg_ref, g_ref)
        wb_ref[round_slot, 1] = _norm_weight(wu_ref, g_ref)
        side_out_ref[...] = side_ref[...].astype(BF16)
        x = x_ref[...]
        gate = _scaled(jnp.dot(x, wb_ref[use_slot, 0], preferred_element_type=F32), rstd_ref)
        up = _scaled(jnp.dot(x, wb_ref[use_slot, 1], preferred_element_type=F32), rstd_ref)
        o_ref[...] = (gate * jax.nn.sigmoid(gate) * up).astype(o_ref.dtype)

    _staged_step(t, nj, wb_ref, body)


def norm_swiglu_up(hb, ss, gain, w_gate, w_up, layer, side):
    m, k = hb.shape
    n = w_gate.shape[2]
    tm = _divisor(m, _ROW_TILES)
    tn = _divisor(n, (256, 128))
    nj = n // tn
    n_steps = (m // tm) * nj + 1
    row, _, out, wcol = _staged_maps(nj, n_steps)
    x_bytes, w_bytes = tm * k * 2, k * tn * 4
    streamed = [2 * w_bytes, tm * tn * 2, tm * LANE * 4, k * LANE * 4]
    resident = tm * LANE * 4 + 2 * w_bytes
    single = not _fits([x_bytes] + streamed, resident)
    side_in, side_out = _rider_specs(side, layer, n_steps)
    return pl.pallas_call(
        functools.partial(_norm_swiglu_kernel, nj=nj),
        out_shape=(jax.ShapeDtypeStruct((m, n), BF16), jax.ShapeDtypeStruct(side.shape[1:], BF16)),
        grid=(n_steps,),
        in_specs=[_row_spec((tm, k), row, single),
                  pl.BlockSpec((tm, LANE), row),
                  pl.BlockSpec((k, 1), lambda t: (0, 0)),
                  _stacked(w_gate, layer, (k, tn), wcol),
                  _stacked(w_up, layer, (k, tn), wcol),
                  side_in],
        out_specs=(pl.BlockSpec((tm, tn), out), side_out),
        scratch_shapes=[pltpu.VMEM((tm, LANE), F32), pltpu.VMEM((2, 2, k, tn), BF16)],
        compiler_params=_params("arbitrary"),
        name="swiglu_up",
    )(hb, ss, gain.reshape(k, 1).astype(F32), w_gate, w_up, side)


def _emit_stream(h_new, first_column, h_ref, hb_ref, ss_ref):
    h_ref[...] = h_new
    hb_ref[...] = h_new.astype(BF16)
    part = _lane_partial_sums(h_new * h_new)

    @pl.when(first_column)
    def _():
        ss_ref[...] = part

    @pl.when(jnp.logical_not(first_column))
    def _():
        ss_ref[...] += part


def _residual_mm_kernel(x_ref, w_ref, r_ref, *out_refs):
    h_new = r_ref[...] + jnp.dot(x_ref[...], w_ref[...], preferred_element_type=F32)
    if len(out_refs) == 1:
        out_refs[0][...] = h_new
    else:
        _emit_stream(h_new, pl.program_id(1) == 0, *out_refs)


def residual_matmul(x, w, res, name, emit_norm=True, drop_head=None):
    k = x.shape[1]
    n = w.shape[1]
    if drop_head is None:
        m = x.shape[0]
        tm = _divisor(m, _ROW_TILES[2:])
        row_of = None
    else:
        batch, rows, head_rows = drop_head
        seq = rows - head_rows
        m = batch * seq
        tm = _divisor(seq, _ROW_TILES[2:])
        per_seq = seq // tm
        row_of = lambda i: pl.multiple_of((i // per_seq) * rows + head_rows + (i % per_seq) * tm,
                                          SUBLANE)
    plans = [(tn, single) for single in (False, True) for tn in (512, 256) if n % tn == 0]
    for tn, single in plans:
        blocks = [k * tn * 2, tm * tn * 4, tm * tn * 4] + ([tm * tn * 2, tm * LANE * 4] if emit_norm else [])
        x_bytes = tm * k * 2
        if _fits(blocks + ([] if single else [x_bytes]), x_bytes if single else 0):
            break
    tile = pl.BlockSpec((tm, tn), lambda i, j: (i, j))
    if row_of is None:
        x_spec = _row_spec((tm, k), lambda i, j: (i, 0), single)
        res_spec = tile
    else:
        x_spec = _row_spec((pl.Element(tm), pl.Element(k)), lambda i, j: (row_of(i), 0), single)
        res_spec = pl.BlockSpec((pl.Element(tm), pl.Element(tn)), lambda i, j: (row_of(i), j * tn))
    out_shape = [jax.ShapeDtypeStruct((m, n), F32)]
    out_specs = [tile]
    if emit_norm:
        out_shape += [jax.ShapeDtypeStruct((m, n), BF16), jax.ShapeDtypeStruct((m, LANE), F32)]
        out_specs += [tile, pl.BlockSpec((tm, LANE), lambda i, j: (i, 0))]
    return pl.pallas_call(
        _residual_mm_kernel,
        out_shape=tuple(out_shape),
        grid=(m // tm, n // tn),
        in_specs=[x_spec, pl.BlockSpec((k, tn), lambda i, j: (0, j)), res_spec],
        out_specs=tuple(out_specs),
        compiler_params=_params("parallel", "arbitrary"),
        name=name,
    )(x, w, res)


def _residual_mm2_kernel(xa_ref, xb_ref, wa_ref, wb_ref, r_ref, h_ref, hb_ref, ss_ref):
    acc = jnp.dot(xa_ref[...], wa_ref[...], preferred_element_type=F32)
    acc = acc + jnp.dot(xb_ref[...], wb_ref[...], preferred_element_type=F32)
    _emit_stream(r_ref[...] + acc, pl.program_id(1) == 0, h_ref, hb_ref, ss_ref)


def residual_matmul2(xa, xb, w, res, name):
    m, kx = xa.shape
    k, n = w.shape
    assert xb.shape == (m, kx) and k == 2 * kx
    tm = _divisor(m, _ROW_TILES)
    tn = _divisor(n, (512, 256, 128))
    tile = pl.BlockSpec((tm, tn), lambda i, j: (i, j))
    return pl.pallas_call(
        _residual_mm2_kernel,
        out_shape=(jax.ShapeDtypeStruct((m, n), F32), jax.ShapeDtypeStruct((m, n), BF16),
                   jax.ShapeDtypeStruct((m, LANE), F32)),
        grid=(m // tm, n // tn),
        in_specs=[pl.BlockSpec((tm, kx), lambda i, j: (i, 0)),
                  pl.BlockSpec((tm, kx), lambda i, j: (i, 0)),
                  pl.BlockSpec((kx, tn), lambda i, j: (0, j)),
                  pl.BlockSpec((kx, tn), lambda i, j: (1, j)),
                  tile],
        out_specs=(tile, tile, pl.BlockSpec((tm, LANE), lambda i, j: (i, 0))),
        compiler_params=_params("parallel", "arbitrary"),
        name=name,
    )(xa, xb, w, w, res)


def _forget_logit_kernel(x_ref, ss_ref, g_ref, w_ref, b_ref, o_ref):
    acc = jnp.dot(x_ref[...], _norm_weight(w_ref, g_ref), preferred_element_type=F32)
    f = acc * _row_scale(ss_ref, x_ref.shape[1]) + b_ref[...]
    o_ref[...] = jnp.minimum(f, 0.0) - jnp.log1p(jnp.exp(-jnp.abs(f)))


def forget_log_gates(hb, ss, gain, w_f, b_f):
    m, k = hb.shape
    n = w_f.shape[1]
    assert n == LANE
    tm = _divisor(m, _ROW_TILES)
    return pl.pallas_call(
        _forget_logit_kernel,
        out_shape=jax.ShapeDtypeStruct((m, n), F32),
        grid=(m // tm,),
        in_specs=[pl.BlockSpec((tm, k), lambda i: (i, 0)),
                  pl.BlockSpec((tm, LANE), lambda i: (i, 0)),
                  pl.BlockSpec((k, 1), lambda i: (0, 0)),
                  pl.BlockSpec((k, n), lambda i: (0, 0)),
                  pl.BlockSpec((1, n), lambda i: (0, 0))],
        out_specs=pl.BlockSpec((tm, n), lambda i: (i, 0)),
        compiler_params=_params("parallel"),
        name="fox_forget_gates",
    )(hb, ss, gain.reshape(k, 1).astype(F32), w_f, b_f)


def _prefix_sum_kernel(x_ref, o_ref, *, lead):
    x = x_ref[0]
    rows = x.shape[0]
    row = lax.broadcasted_iota(jnp.int32, x.shape, 0)
    x = jnp.where(row >= lead, x, 0.0)
    shift = 1
    while shift < rows:
        x = x + jnp.where(row >= shift, pltpu.roll(x, shift, 0), 0.0)
        shift *= 2
    o_ref[0] = x


def prefix_sum_time(x, lead):
    b, rows, lanes = x.shape
    return pl.pallas_call(
        functools.partial(_prefix_sum_kernel, lead=lead),
        out_shape=jax.ShapeDtypeStruct(x.shape, F32),
        grid=(b,),
        in_specs=[pl.BlockSpec((1, rows, lanes), lambda i: (i, 0, 0))],
        out_specs=pl.BlockSpec((1, rows, lanes), lambda i: (i, 0, 0)),
        compiler_params=_params("parallel"),
        name="fox_prefix_sum",
    )(x)


def _split3(x):
    hi = x.astype(BF16).astype(F32)
    rem = x - hi
    mid = rem.astype(BF16).astype(F32)
    lo = (rem - mid).astype(BF16).astype(F32)
    return hi, mid, lo


def _fox_kernel(q_ref, k_ref, v_ref, cum_ref, o_ref, qb_ref, kb_ref, m_ref, acc_ref,
                *, lead, head_rows, tq):
    h = pl.program_id(1)
    rows, dh = q_ref.shape[1], q_ref.shape[2]
    n_main = (rows - head_rows) // tq
    lanes = cum_ref.shape[2]

    def build_bias(r0, n):
        lane = lax.broadcasted_iota(jnp.int32, (n, lanes), 1)
        c = jnp.sum(jnp.where(lane == h, cum_ref[0, pl.ds(r0, n), :], 0.0),
                    axis=-1, keepdims=True) * LOG2E
        hi, mid, lo = _split3(c)
        is_pad = (r0 + lax.broadcasted_iota(jnp.int32, (n, 1), 0)) < lead
        qb = jnp.where(lane == 0, hi, jnp.where(lane == 1, mid, jnp.where(lane == 2, lo,
             jnp.where(lane < 7, 1.0, 0.0))))
        kb = jnp.where(lane < 3, 1.0, jnp.where(lane == 3, -hi, jnp.where(lane == 4, -mid,
             jnp.where(lane == 5, -lo, jnp.where(jnp.logical_and(lane == 6, is_pad),
                                                  MASK_VALUE, 0.0)))))
        qb_ref[pl.ds(r0, n), :] = qb.astype(BF16)
        kb_ref[pl.ds(r0, n), :] = kb.astype(BF16)

    build_bias(0, head_rows)
    for i in range(n_main):
        build_bias(head_rows + i * tq, tq)

    def row_max(s):
        return jnp.broadcast_to(jnp.max(s, axis=-1, keepdims=True), (s.shape[0], lanes))

    def scores(q_aug, k0, nk):
        k_aug = jnp.concatenate([k_ref[0, pl.ds(k0, nk), :], kb_ref[pl.ds(k0, nk), :]], axis=1)
        return lax.dot_general(q_aug, k_aug, (((1,), (1,)), ((), ())),
                               preferred_element_type=F32)

    def values(k0, nk):
        return jnp.concatenate([v_ref[0, pl.ds(k0, nk), :], jnp.ones((nk, dh), BF16)], axis=1)

    def accumulate(s, s_max, v_aug):
        nq, nk = s.shape
        m_prev = m_ref[pl.ds(0, nq), :]
        m_new = jnp.maximum(m_prev, s_max)
        alpha = jnp.exp2(m_prev - m_new)
        p = jnp.exp2(s - jnp.tile(m_new, (1, nk // lanes)))
        acc_ref[pl.ds(0, nq), :] = (jnp.tile(alpha, (1, 2)) * acc_ref[pl.ds(0, nq), :]
                                    + jnp.dot(p.astype(BF16), v_aug, preferred_element_type=F32))
        m_ref[pl.ds(0, nq), :] = m_new

    def causal(s):
        return jnp.where(lax.broadcasted_iota(jnp.int32, s.shape, 1)
                         <= lax.broadcasted_iota(jnp.int32, s.shape, 0), s, MASK_VALUE)

    def reset(nq):
        m_ref[pl.ds(0, nq), :] = jnp.full((nq, lanes), MASK_VALUE, F32)
        acc_ref[pl.ds(0, nq), :] = jnp.zeros((nq, 2 * dh), F32)

    def finish(r0, nq):
        o_ref[0, pl.ds(r0, nq), :] = (acc_ref[pl.ds(0, nq), pl.ds(0, dh)]
                                      / acc_ref[pl.ds(0, nq), pl.ds(dh, dh)]).astype(o_ref.dtype)

    def q_rows(r0, nq):
        return jnp.concatenate([q_ref[0, pl.ds(r0, nq), :], qb_ref[pl.ds(r0, nq), :]], axis=1)

    reset(head_rows)
    s_head = causal(scores(q_rows(0, head_rows), 0, head_rows))
    accumulate(s_head, row_max(s_head), values(0, head_rows))
    finish(0, head_rows)

    def q_tile(i, carry):
        r0 = pl.multiple_of(head_rows + i * tq, LANE)
        q_aug = q_rows(r0, tq)
        reset(tq)

        def kv_tile(j, s):
            k0 = pl.multiple_of(head_rows + j * tq, LANE)
            s_next = scores(q_aug, pl.multiple_of(k0 + tq, LANE), tq)
            accumulate(s, row_max(s), values(k0, tq))
            return s_next

        s_diag = lax.fori_loop(0, i, kv_tile, scores(q_aug, head_rows, tq))
        s_last = jnp.concatenate([causal(s_diag), scores(q_aug, 0, head_rows)], axis=1)
        v_last = jnp.concatenate([values(r0, tq), values(0, head_rows)], axis=0)
        accumulate(s_last, row_max(s_last), v_last)
        finish(r0, tq)
        return carry

    lax.fori_loop(0, n_main, q_tile, 0)


def fox_attention(z, cum, *, batch, rows, lead, head_rows, heads, head_dim, q_col, k_col, v_col):
    z3 = z.reshape(batch, rows, z.shape[1])
    lanes = cum.shape[2]
    assert head_dim == lanes and head_rows == lanes
    tq = _divisor(rows - head_rows, (512, 256, 128))
    qb, kb, vb = q_col // head_dim, k_col // head_dim, v_col // head_dim
    kern = functools.partial(_fox_kernel, lead=lead, head_rows=head_rows, tq=tq)
    tmax = max(tq, head_rows)
    return pl.pallas_call(
        kern,
        out_shape=jax.ShapeDtypeStruct((batch, rows, heads * head_dim), BF16),
        grid=(batch, heads),
        in_specs=[pl.BlockSpec((1, rows, head_dim), lambda b, h: (b, 0, qb + h)),
                  pl.BlockSpec((1, rows, head_dim), lambda b, h: (b, 0, kb + h)),
                  pl.BlockSpec((1, rows, head_dim), lambda b, h: (b, 0, vb + h)),
                  pl.BlockSpec((1, rows, lanes), lambda b, h: (b, 0, 0))],
        out_specs=pl.BlockSpec((1, rows, head_dim), lambda b, h: (b, 0, h)),
        scratch_shapes=[pltpu.VMEM((rows, lanes), BF16), pltpu.VMEM((rows, lanes), BF16),
                        pltpu.VMEM((tmax, lanes), F32), pltpu.VMEM((tmax, 2 * head_dim), F32)],
        compiler_params=_params("parallel", "arbitrary"),
        name="fox_attention",
    )(z3, z3, z3, cum)


def _lru_kernel(x_ref, gate_ref, cw_ref, cb_ref, wa_ref, ba_ref, wx_ref, bx_ref, lam_ref,
                o_ref, ext_ref, h_ref, carry_ref, *, lead):
    t = pl.program_id(2)
    tl, tc = x_ref.shape[1], x_ref.shape[2]
    taps = cw_ref.shape[0]
    seg = tl // SUBLANE
    n_slab = tc // LANE
    top = min(tl, -(-lead // SUBLANE) * SUBLANE)

    @pl.when(t == 0)
    def _():
        ext_ref[:, pl.ds(0, SUBLANE), :] = jnp.zeros((n_slab, SUBLANE, LANE), F32)
        carry_ref[...] = jnp.zeros_like(carry_ref)

    x = x_ref[0].astype(F32)
    row_top = t * tl + lax.broadcasted_iota(jnp.int32, (top, 1), 0)
    x = jnp.concatenate([jnp.where(row_top >= lead, x[:top], 0.0), x[top:]], axis=0)
    lam = lam_ref[...]
    rate = -LRU_C * (jnp.maximum(-lam, 0.0) + jnp.log1p(jnp.exp(-jnp.abs(lam))))
    sub = lax.broadcasted_iota(jnp.int32, (SUBLANE, 1), 0)

    for c in range(n_slab):
        cols = slice(c * LANE, (c + 1) * LANE)
        ext_ref[c, pl.ds(SUBLANE, tl), :] = x[:, cols]

        def rows_at(g):
            return ext_ref[c, pl.ds(SUBLANE + g, SUBLANE, stride=seg), :]

        shifted = {g: rows_at(g) for g in range(-(taps - 1), seg)}
        conv = []
        for g in range(seg):
            acc = cb_ref[:, cols] + shifted[g - (taps - 1)] * cw_ref[pl.ds(0, 1), cols]
            for j in range(1, taps):
                acc = acc + shifted[g - (taps - 1) + j] * cw_ref[pl.ds(j, 1), cols]
            conv.append(acc)
        xc = jnp.concatenate(conv, axis=0)
        ext_ref[c, pl.ds(0, SUBLANE), :] = ext_ref[c, pl.ds(tl, SUBLANE), :]

        xcb = xc.astype(BF16)
        r = jax.nn.sigmoid(jnp.dot(xcb, wa_ref[c], preferred_element_type=F32) + ba_ref[:, cols])
        i = jax.nn.sigmoid(jnp.dot(xcb, wx_ref[c], preferred_element_type=F32) + bx_ref[:, cols])
        log_a = rate[:, cols] * r
        a = jnp.exp(log_a)
        b = jnp.sqrt(1.0 - jnp.exp(2.0 * log_a)) * (i * xc)

        h_loc, prod = [], []
        for g in range(seg):
            rows = slice(g * SUBLANE, (g + 1) * SUBLANE)
            b_g = jnp.where(t * tl + sub * seg + g >= lead, b[rows], 0.0)
            h_loc.append(b_g if g == 0 else a[rows] * h_loc[-1] + b_g)
            prod.append(a[rows] if g == 0 else a[rows] * prod[-1])

        state = [carry_ref[:, cols]]
        for s_ in range(SUBLANE):
            state.append(prod[-1][s_:s_ + 1] * state[-1] + h_loc[-1][s_:s_ + 1])
        carry_ref[:, cols] = state[SUBLANE]
        entering = jnp.concatenate(state[:SUBLANE], axis=0)
        for g in range(seg):
            h_ref[c, pl.ds(g, SUBLANE, stride=seg), :] = h_loc[g] + prod[g] * entering

    gate = gate_ref[0].astype(F32)
    for c in range(n_slab):
        cols = slice(c * LANE, (c + 1) * LANE)
        o_ref[0, :, cols] = (h_ref[c] * jax.nn.gelu(gate[:, cols])).astype(o_ref.dtype)


def lru_branch(z, conv_w, conv_b, w_a, b_a, w_x, b_x, lam, *, batch, rows, lead, width):
    z3 = z.reshape(batch, rows, z.shape[1])
    nblk, bd, _ = w_a.shape
    assert bd == LANE
    tc = _divisor(width, (512, 256, 128))
    tl = _divisor(rows, (352, 96, 32, 8))
    nct = width // tc
    row2 = lambda v: v.reshape(1, width).astype(F32)
    kern = functools.partial(_lru_kernel, lead=lead)
    vec = pl.BlockSpec((1, tc), lambda b, c, t: (0, c))
    return pl.pallas_call(
        kern,
        out_shape=jax.ShapeDtypeStruct((batch, rows, width), BF16),
        grid=(batch, nct, rows // tl),
        in_specs=[pl.BlockSpec((1, tl, tc), lambda b, c, t: (b, t, c)),
                  pl.BlockSpec((1, tl, tc), lambda b, c, t: (b, t, nct + c)),
                  pl.BlockSpec((conv_w.shape[0], tc), lambda b, c, t: (0, c)),
                  vec,
                  pl.BlockSpec((tc // bd, bd, bd), lambda b, c, t: (c, 0, 0)),
                  vec,
                  pl.BlockSpec((tc // bd, bd, bd), lambda b, c, t: (c, 0, 0)),
                  vec, vec],
        out_specs=pl.BlockSpec((1, tl, tc), lambda b, c, t: (b, t, c)),
        scratch_shapes=[pltpu.VMEM((tc // LANE, tl + SUBLANE, LANE), F32),
                        pltpu.VMEM((tc // LANE, tl, LANE), F32), pltpu.VMEM((1, tc), F32)],
        compiler_params=_params("parallel", "parallel", "arbitrary"),
        name="rg_lru",
    )(z3, z3, conv_w.astype(F32), row2(conv_b), w_a.astype(BF16), row2(b_a),
      w_x.astype(BF16), row2(b_x), row2(lam))


def _retention_kernel(lg_ref, q_ref, k_ref, v_ref, g_ref, cos_ref, sin_ref, gain_ref, o_ref,
                      s_ref, d_ref, *, lead, tb, chunk):
    h = pl.program_id(1)
    rows = q_ref.shape[1]
    dk = q_ref.shape[2]
    half = dk // 2
    lg = lg_ref[h]

    ti = lax.broadcasted_iota(jnp.int32, (tb, tb), 0)
    si = lax.broadcasted_iota(jnp.int32, (tb, tb), 1)
    dist = jnp.abs(ti - si).astype(F32)
    d_ref[...] = jnp.where(si // chunk <= ti // chunk, jnp.exp(lg * dist), 0.0)
    idx = lax.broadcasted_iota(jnp.int32, (tb, 1), 0).astype(F32)
    q_decay = jnp.exp(lg * (idx + 1.0))
    k_decay = jnp.exp(lg * (tb - 1.0 - idx))
    block_decay = jnp.exp(lg * jnp.full((1, 1), float(tb), F32))
    s_ref[...] = jnp.zeros_like(s_ref)

    def rotate(x, c, s):
        x1, x2 = x[:, :half], x[:, half:]
        return jnp.concatenate([x1 * c - x2 * s, x1 * s + x2 * c], axis=1)

    def block(n, carry):
        r0 = pl.multiple_of(n * tb, LANE)
        valid = (r0 + lax.broadcasted_iota(jnp.int32, (tb, 1), 0)) >= lead
        c = cos_ref[pl.ds(r0, tb), :]
        s = sin_ref[pl.ds(r0, tb), :]
        q = rotate(q_ref[0, pl.ds(r0, tb), :].astype(F32), c, s)
        k = rotate(k_ref[0, pl.ds(r0, tb), :].astype(F32), c, s) * (dk ** -0.5)
        k = jnp.where(valid, k, 0.0)
        v = v_ref[0, pl.ds(r0, tb), :]
        v = jnp.where(valid, v, jnp.zeros_like(v))

        w = lax.dot_general(q.astype(BF16), k.astype(BF16), (((1,), (1,)), ((), ())),
                            preferred_element_type=F32) * d_ref[...]
        state = s_ref[...]
        o = jnp.dot(w.astype(BF16), v, preferred_element_type=F32)
        o = o + jnp.dot((q * q_decay).astype(BF16), state.astype(BF16),
                        preferred_element_type=F32)
        s_ref[...] = state * block_decay + lax.dot_general(
            (k * k_decay).astype(BF16), v, (((0,), (0,)), ((), ())),
            preferred_element_type=F32)

        ms = jnp.mean(o * o, axis=-1, keepdims=True)
        on = o * lax.rsqrt(ms + RMS_EPS) * gain_ref[0]
        g = g_ref[0, pl.ds(r0, tb), :].astype(F32)
        o_ref[0, pl.ds(r0, tb), :] = (g * jax.nn.sigmoid(g) * on).astype(o_ref.dtype)
        return carry

    lax.fori_loop(0, rows // tb, block, 0)


def retention(z, cos, sin, ret_norm, *, batch, rows, lead, heads):
    cols = z.shape[1]
    z3 = z.reshape(batch, rows, cols)
    vw = ret_norm.shape[0]
    qk = (cols - 2 * vw) // 2
    dk, dv = qk // heads, vw // heads
    tb = _divisor(rows, (384, 128, 64))
    assert tb % RET_CHUNK == 0
    log_gamma = jnp.log(1.0 - 2.0 ** (-5.0 - jnp.arange(heads, dtype=F32)))
    kern = functools.partial(_retention_kernel, lead=lead, tb=tb, chunk=RET_CHUNK)
    kb, vb, gb = qk // dk, 2 * qk // dv, (2 * qk + vw) // dv
    return pl.pallas_call(
        kern,
        out_shape=jax.ShapeDtypeStruct((batch, rows, vw), BF16),
        grid=(batch, heads),
        in_specs=[pl.BlockSpec(memory_space=pltpu.SMEM),
                  pl.BlockSpec((1, rows, dk), lambda b, h: (b, 0, h)),
                  pl.BlockSpec((1, rows, dk), lambda b, h: (b, 0, kb + h)),
                  pl.BlockSpec((1, rows, dv), lambda b, h: (b, 0, vb + h)),
                  pl.BlockSpec((1, rows, dv), lambda b, h: (b, 0, gb + h)),
                  pl.BlockSpec((rows, dk // 2), lambda b, h: (0, 0)),
                  pl.BlockSpec((rows, dk // 2), lambda b, h: (0, 0)),
                  pl.BlockSpec((1, 1, dv), lambda b, h: (h, 0, 0))],
        out_specs=pl.BlockSpec((1, rows, dv), lambda b, h: (b, 0, h)),
        scratch_shapes=[pltpu.VMEM((dk, dv), F32), pltpu.VMEM((tb, tb), F32)],
        compiler_params=_params("parallel", "arbitrary"),
        name="retention",
    )(log_gamma, z3, z3, z3, z3, cos, sin, ret_norm.reshape(heads, 1, dv).astype(F32))


def _ffn(stream, layer, norm, w_gate, w_up, w_down, last):
    h, hb, ss = stream
    a, w_down_bf16 = norm_swiglu_up(hb, ss, norm, w_gate, w_up, layer, w_down)
    return residual_matmul(a, w_down_bf16, h, "ffn_down", emit_norm=not last)


def _lru_fox_layer(stream, geom, j, norm, w_in_all, b_f, conv_w, conv_b, w_a, b_a, w_x, b_x, lam,
                   q_norm, k_norm, w_out_all):
    h, hb, ss = stream
    batch, rows, lead, head_rows = geom
    w_in = w_in_all[j]
    heads = b_f.shape[0]
    head_dim = q_norm.shape[0]
    fox_w = heads * head_dim
    lru_w = lam.shape[0]
    q_col, k_col, v_col, f_col = 2 * lru_w, 2 * lru_w + fox_w, 2 * lru_w + 2 * fox_w, 2 * lru_w + 3 * fox_w

    gains = jnp.ones((1, f_col), F32)
    gains = gains.at[0, q_col:k_col].set(
        jnp.tile(q_norm.astype(F32) * (head_dim ** -0.5 * LOG2E), heads))
    gains = gains.at[0, k_col:v_col].set(jnp.tile(k_norm.astype(F32), heads))
    w_in_bf16 = (w_in[:, :f_col] * norm.astype(F32)[:, None]).astype(BF16)
    z = headnorm_matmul(hb, ss, w_in_bf16, gains, (q_col, v_col), head_dim, "lru_fox_in_proj")

    w_f = jnp.zeros((w_in.shape[0], LANE), F32).at[:, :heads].set(w_in[:, f_col:])
    bias_f = jnp.zeros((1, LANE), F32).at[0, :heads].set(b_f.astype(F32))
    log_f = forget_log_gates(hb, ss, norm, w_f, bias_f).reshape(batch, rows, LANE)
    cum = prefix_sum_time(log_f, lead)

    y_lru = lru_branch(z, conv_w, conv_b, w_a, b_a, w_x, b_x, lam,
                       batch=batch, rows=rows, lead=lead, width=lru_w)
    y_fox = fox_attention(z, cum, batch=batch, rows=rows, lead=lead, head_rows=head_rows,
                          heads=heads, head_dim=head_dim, q_col=q_col, k_col=k_col, v_col=v_col)
    m = batch * rows
    return residual_matmul2(y_lru.reshape(m, lru_w), y_fox.reshape(m, fox_w),
                            w_out_all[j].astype(BF16), h, "lru_fox_out_proj")


def _retention_layer(stream, geom, j, norm, w_in_all, ret_norm, w_out_all, last):
    h, hb, ss = stream
    batch, rows, lead, _ = geom
    vw = ret_norm.shape[0]
    n_in = w_in_all.shape[2]
    dk = (n_in - 2 * vw) // 2 // RET_HEADS
    z, w_out_bf16 = norm_matmul(hb, ss, norm, w_in_all, j, w_out_all, "retention_in_proj")
    half = dk // 2
    inv = ROPE_BASE ** (-jnp.arange(half, dtype=F32) / half)
    pos = (jnp.arange(rows) - lead).astype(F32)
    ang = pos[:, None] * inv[None, :]
    y = retention(z, jnp.cos(ang), jnp.sin(ang), ret_norm, batch=batch, rows=rows, lead=lead,
                  heads=RET_HEADS)
    return residual_matmul(y.reshape(batch * rows, vw), w_out_bf16, h, "retention_out_proj",
                           drop_head=geom[:2] + geom[3:] if last else None)


def kernel(x, meta_tokens, ab_norm, ab_w_in, ab_b_f, ab_conv_w, ab_conv_b, ab_w_a, ab_b_a, ab_w_x, ab_b_x, ab_lambda, ab_q_norm, ab_k_norm, ab_w_out, c_norm, c_w_in, c_ret_norm, c_w_out, ffn_norm, ffn_w_gate, ffn_w_up, ffn_w_down):
    batch, seq, d = x.shape
    n_meta = meta_tokens.shape[0]
    depth = ffn_norm.shape[0]
    lead = (-n_meta) % LANE
    head_rows = lead + n_meta
    rows = head_rows + seq
    assert seq % LANE == 0 and lead % RET_CHUNK == (-n_meta) % RET_CHUNK
    geom = (batch, rows, lead, head_rows)
    m = batch * rows

    head = jnp.concatenate([jnp.zeros((lead, d), F32), meta_tokens.astype(F32)], axis=0)
    stream = tuple(a.reshape(m, a.shape[-1]) for a in embed(x.astype(F32), head))
    for layer in range(depth):
        j = layer // 2
        if layer % 2 == 0:
            stream = _lru_fox_layer(stream, geom, j, ab_norm[j], ab_w_in, ab_b_f[j], ab_conv_w[j],
                                    ab_conv_b[j], ab_w_a[j], ab_b_a[j], ab_w_x[j], ab_b_x[j],
                                    ab_lambda[j], ab_q_norm[j], ab_k_norm[j], ab_w_out)
        else:
            stream = _retention_layer(stream, geom, j, c_norm[j], c_w_in, c_ret_norm[j], c_w_out,
                                      last=layer == depth - 1)
        stream = _ffn(stream, layer, ffn_norm[layer], ffn_w_gate, ffn_w_up, ffn_w_down,
                      last=layer == depth - 1)
    h = stream[0]
    if h.shape[0] == batch * seq:
        return h.reshape(batch, seq, d)
    return h.reshape(batch, rows, d)[:, head_rows:]
```
